```python
import jax, jax.numpy as jnp
from jax import lax
import numpy as np

D_MODEL = 2048
BATCH = 4
SEQ = 2048
DEPTH = 1
DEC_BATCH = 128
DEC_SEQ = 1
PAST_LEN = 16384
PAGE_SIZE = 128

N_META = 16
MIX_WIDTH = D_MODEL
A_WIDTH = MIX_WIDTH // 2
B_WIDTH = MIX_WIDTH - A_WIDTH
A_EXPAND = 128
A_HEADS = A_WIDTH // A_EXPAND
A_DK = A_EXPAND
A_DV = A_WIDTH // A_HEADS
CHUNK = 64
CONV_W = 31
B_GROUPS = 8
D_FF = -(-8 * D_MODEL // (3 * 256)) * 256
IN_COLS = 4 * A_WIDTH + 2 * B_WIDTH
EPS = 1e-6

kernel_name = "hymba_hgrn2_conformer_conv_decode_step"


def _rmsnorm(x, g):
    xf = x.astype(jnp.float32)
    y = xf * lax.rsqrt(jnp.mean(xf * xf, axis=-1, keepdims=True) + EPS)
    return (y * g.astype(jnp.float32)).astype(x.dtype)


def _gla_chunk(S0, q, k, v, logf):
    C = q.shape[2]
    b = jnp.cumsum(logf, axis=2)
    mask = jnp.tril(jnp.ones((C, C), dtype=bool))[None, None, :, :, None]
    diff = b[:, :, :, None, :] - b[:, :, None, :, :]
    decay = jnp.exp(jnp.where(mask, diff, -jnp.inf))
    scores = jnp.einsum('bhtk,bhsk,bhtsk->bhts', q, k, decay)
    o = jnp.einsum('bhts,bhsv->bhtv', scores, v) + \
        jnp.einsum('bhtk,bhkv->bhtv', q * jnp.exp(b), S0)
    b_last = b[:, :, -1, :]
    S_new = jnp.exp(b_last)[..., None] * S0 + \
        jnp.einsum('bhsk,bhsv->bhkv', k * jnp.exp(b_last[:, :, None, :] - b), v)
    return o, S_new


def _hgrn_prompt(q, k, v, logf):
    B, H, T, _ = q.shape
    S0 = jnp.zeros((B, H, A_DK, A_DV), jnp.float32)
    o_meta, S = _gla_chunk(S0, q[:, :, :N_META], k[:, :, :N_META], v[:, :, :N_META], logf[:, :, :N_META])
    n_chunks = (T - N_META) // CHUNK

    def to_chunks(t):
        t = t[:, :, N_META:]
        return t.reshape(B, H, n_chunks, CHUNK, t.shape[-1]).transpose(2, 0, 1, 3, 4)

    def step(S_c, xs):
        qc, kc, vc, fc = xs
        o_c, S_n = _gla_chunk(S_c, qc, kc, vc, fc)
        return S_n, o_c

    S, o_rest = lax.scan(step, S, (to_chunks(q), to_chunks(k), to_chunks(v), to_chunks(logf)))
    o_rest = o_rest.transpose(1, 2, 0, 3, 4).reshape(B, H, n_chunks * CHUNK, A_DV)
    return jnp.concatenate([o_meta, o_rest], axis=2), S


def _layer(h, conv_buf, S0, is_prompt, lb, g_mix, w_in, hgrn_g, conv_w, conv_b, gn_g, gn_b,
           w_out, g_ffn, w_gate, w_up, w_down):
    B, T, _ = h.shape
    f32 = jnp.float32
    z = _rmsnorm(h, g_mix) @ w_in
    q, f, i, og, ga, gb = jnp.split(
        z, [A_WIDTH, 2 * A_WIDTH, 3 * A_WIDTH, 4 * A_WIDTH, 4 * A_WIDTH + B_WIDTH], axis=-1)

    def heads(t, d):
        return t.reshape(B, T, A_HEADS, d).transpose(0, 2, 1, 3).astype(f32)

    fgate = lb + (1.0 - lb) * jax.nn.sigmoid(f.astype(f32))
    qh = jax.nn.silu(heads(q, A_DK))
    kh = heads(1.0 - fgate, A_DK)
    logf = heads(jnp.log(fgate), A_DK)
    vh = heads(i, A_DV)
    if is_prompt:
        o, S_new = _hgrn_prompt(qh, kh, vh, logf)
    else:
        o, S_new = _gla_chunk(S0.astype(f32), qh, kh, vh, logf)
    o = o.transpose(0, 2, 1, 3)
    o = o * lax.rsqrt(jnp.mean(o * o, axis=-1, keepdims=True) + EPS) * hgrn_g.astype(f32).reshape(A_HEADS, A_DV)
    o_a = o.reshape(B, T, A_WIDTH) * jax.nn.silu(og.astype(f32))

    u = ga.astype(f32) * jax.nn.sigmoid(gb.astype(f32))
    ucat = jnp.concatenate([conv_buf.astype(f32), u], axis=1)
    c = lax.conv_general_dilated(ucat, conv_w.astype(f32)[:, None, :], window_strides=(1,),
                                 padding='VALID', dimension_numbers=('NWC', 'WIO', 'NWC'),
                                 feature_group_count=B_WIDTH) + conv_b.astype(f32)
    new_buf = ucat[:, -(CONV_W - 1):]
    cg = c.reshape(B, T, B_GROUPS, B_WIDTH // B_GROUPS)
    mu = jnp.mean(cg, axis=-1, keepdims=True)
    var = jnp.mean(jnp.square(cg - mu), axis=-1, keepdims=True)
    cn = ((cg - mu) * lax.rsqrt(var + EPS)).reshape(B, T, B_WIDTH) * gn_g.astype(f32) + gn_b.astype(f32)
    o_b = jax.nn.silu(cn)

    h = h + jnp.concatenate([o_a, o_b], axis=-1).astype(h.dtype) @ w_out

    hf = _rmsnorm(h, g_ffn)
    h = h + (jax.nn.silu(hf @ w_gate) * (hf @ w_up)) @ w_down
    return h, S_new, new_buf


def setup_inputs(seed: int = 0) -> dict:
    key = jax.random.key(seed)
    ks = jax.random.split(key, 20)
    n = jax.random.normal
    return {
        "x_prompt": n(ks[0], (BATCH, SEQ, D_MODEL), jnp.float32),
        "x_sample": n(ks[1], (DEC_BATCH, DEC_SEQ, D_MODEL), jnp.float32),
        "state_hgrn": 0.3 * n(ks[2], (DEPTH, DEC_BATCH, A_HEADS, A_DK, A_DV), jnp.float32),
        "state_conv": n(ks[3], (DEPTH, DEC_BATCH, CONV_W - 1, B_WIDTH), jnp.float32),
        "meta_tokens": n(ks[4], (N_META, D_MODEL), jnp.float32),
        "norm_mix_g": 1.0 + 0.02 * n(ks[5], (DEPTH, D_MODEL), jnp.float32),
        "w_in": n(ks[6], (DEPTH, D_MODEL, IN_COLS), jnp.float32) * D_MODEL ** -0.5,
        "lb_logits": 0.5 * n(ks[7], (DEPTH + 1, A_WIDTH), jnp.float32),
        "hgrn_norm_g": 1.0 + 0.02 * n(ks[8], (DEPTH, A_WIDTH), jnp.float32),
        "conv_w": n(ks[9], (DEPTH, CONV_W, B_WIDTH), jnp.float32) * CONV_W ** -0.5,
        "conv_b": 0.02 * n(ks[10], (DEPTH, B_WIDTH), jnp.float32),
        "gn_g": 1.0 + 0.02 * n(ks[11], (DEPTH, B_WIDTH), jnp.float32),
        "gn_b": 0.02 * n(ks[12], (DEPTH, B_WIDTH), jnp.float32),
        "w_out": n(ks[13], (DEPTH, MIX_WIDTH, D_MODEL), jnp.float32) * MIX_WIDTH ** -0.5,
        "norm_ffn_g": 1.0 + 0.02 * n(ks[14], (DEPTH, D_MODEL), jnp.float32),
        "w_ffn_gate": n(ks[15], (DEPTH, D_MODEL, D_FF), jnp.float32) * D_MODEL ** -0.5,
        "w_ffn_up": n(ks[16], (DEPTH, D_MODEL, D_FF), jnp.float32) * D_MODEL ** -0.5,
        "w_ffn_down": n(ks[17], (DEPTH, D_FF, D_MODEL), jnp.float32) * D_FF ** -0.5,
        "norm_final_g": 1.0 + 0.02 * n(ks[18], (D_MODEL,), jnp.float32),
    }


def reference(x_prompt, x_sample, state_hgrn, state_conv, meta_tokens, norm_mix_g, w_in, lb_logits,
              hgrn_norm_g, conv_w, conv_b, gn_g, gn_b, w_out, norm_ffn_g, w_ffn_gate, w_ffn_up,
              w_ffn_down, norm_final_g):
    lb_all = jnp.cumsum(jax.nn.softmax(lb_logits.astype(jnp.float32), axis=0), axis=0)

    hp = jnp.concatenate(
        [jnp.broadcast_to(meta_tokens.astype(x_prompt.dtype)[None], (x_prompt.shape[0], N_META, D_MODEL)), x_prompt],
        axis=1)
    hs = x_sample
    zero_buf = jnp.zeros((x_prompt.shape[0], CONV_W - 1, B_WIDTH), jnp.float32)

    sp_list, cp_list, ss_list, cs_list = [], [], [], []
    for l in range(DEPTH):
        p = (lb_all[l], norm_mix_g[l], w_in[l], hgrn_norm_g[l], conv_w[l], conv_b[l], gn_g[l], gn_b[l],
             w_out[l], norm_ffn_g[l], w_ffn_gate[l], w_ffn_up[l], w_ffn_down[l])
        hp, S_p, buf_p = _layer(hp, zero_buf, None, True, *p)
        hs, S_s, buf_s = _layer(hs, state_conv[l], state_hgrn[l], False, *p)
        sp_list.append(S_p.astype(x_prompt.dtype))
        cp_list.append(buf_p.astype(x_prompt.dtype))
        ss_list.append(S_s.astype(x_sample.dtype))
        cs_list.append(buf_s.astype(x_sample.dtype))

    y_prompt = _rmsnorm(hp, norm_final_g)[:, N_META:]
    y_sample = _rmsnorm(hs, norm_final_g)
    new_state_hgrn_prompt = jnp.stack(sp_list)
    new_state_conv_prompt = jnp.stack(cp_list)
    new_state_hgrn_sample = jnp.stack(ss_list)
    new_state_conv_sample = jnp.stack(cs_list)
    return (y_prompt, y_sample, new_state_hgrn_prompt, new_state_conv_prompt, new_state_hgrn_sample, new_state_conv_sample)
```

```python
import functools

import jax
import jax.numpy as jnp
from jax import lax
from jax.experimental import pallas as pl
from jax.experimental.pallas import tpu as pltpu

F32 = jnp.float32
BF16 = jnp.bfloat16

D_MODEL = 2048
N_META = 16
A_WIDTH = 1024
B_WIDTH = 1024
HEAD = 128
A_HEADS = A_WIDTH // HEAD
CONV_W = 31
HIST = CONV_W - 1
B_GROUPS = 8
GROUP = B_WIDTH // B_GROUPS
D_FF = 5632
IN_COLS = 4 * A_WIDTH + 2 * B_WIDTH
EPS = 1e-6

V7X_LANES = 128
V7X_SUBLANES = 8
V7X_VMEM_BYTES = 64 * 1024 * 1024
V7X_VMEM_USABLE = V7X_VMEM_BYTES - 8 * 1024 * 1024

CHUNK = 64
SUB = V7X_SUBLANES
NEG_BIG = -1e30

_NT = (((1,), (1,)), ((), ()))
_TN = (((0,), (0,)), ((), ()))


def _vmem_limit(pipelined_block_bytes, resident_bytes):
    est = 2 * pipelined_block_bytes + resident_bytes
    return int(min(max(est, 16 * 1024 * 1024), V7X_VMEM_USABLE))


def _params(vmem_bytes, ndims):
    return pltpu.CompilerParams(dimension_semantics=("arbitrary",) * ndims,
                                vmem_limit_bytes=vmem_bytes)


def _sigmoid(x):
    return 1.0 / (1.0 + jnp.exp(-x))


def _silu(x):
    return x * _sigmoid(x)


def _rms_rows(x, g):
    return x * lax.rsqrt(jnp.mean(x * x, axis=-1, keepdims=True) + EPS) * g


def _inproj_kernel(x_ref, g_ref, w_ref, z_ref, xn_ref):
    @pl.when(pl.program_id(1) == 0)
    def _():
        xn_ref[...] = _rms_rows(x_ref[...], g_ref[...]).astype(BF16)

    z_ref[...] = jnp.dot(xn_ref[...], w_ref[...], preferred_element_type=F32)


def _inproj(x, g, w_bf16, tm, tn):
    m = x.shape[0]
    n = w_bf16.shape[1]
    blocks = tm * D_MODEL * 4 + D_MODEL * tn * 2 + tm * tn * 4
    resident = tm * D_MODEL * 2 + tm * tn * 4
    return pl.pallas_call(
        _inproj_kernel,
        grid=(m // tm, n // tn),
        in_specs=[pl.BlockSpec((tm, D_MODEL), lambda i, j: (i, 0)),
                  pl.BlockSpec((1, D_MODEL), lambda i, j: (0, 0)),
                  pl.BlockSpec((D_MODEL, tn), lambda i, j: (0, j))],
        out_specs=pl.BlockSpec((tm, tn), lambda i, j: (i, j)),
        out_shape=jax.ShapeDtypeStruct((m, n), F32),
        scratch_shapes=[pltpu.VMEM((tm, D_MODEL), BF16)],
        compiler_params=_params(_vmem_limit(blocks, resident), 2),
        name="inproj",
    )(x, g, w_bf16)


def _lower_bound(lb_logits):
    m = jnp.max(lb_logits, axis=0, keepdims=True)
    e = jnp.exp(lb_logits - m)
    return e[0:1, :] / jnp.sum(e, axis=0, keepdims=True)


def _cumsum_rows(x):
    n = x.shape[0]
    row = lax.broadcasted_iota(jnp.int32, x.shape, 0)
    s = 1
    while s < n:
        x = x + jnp.where(row >= s, pltpu.roll(x, s, 0), 0.0)
        s *= 2
    return x


def _gates(zf, lb):
    fg = lb + (1.0 - lb) * _sigmoid(zf)
    return 1.0 - fg, jnp.log(fg)


def _chunk_masks(c):
    row_w = lax.broadcasted_iota(jnp.int32, (c, HEAD), 0)
    row = lax.broadcasted_iota(jnp.int32, (c, c), 0)
    col = lax.broadcasted_iota(jnp.int32, (c, c), 1)
    masks = []
    half = c // 2
    while half >= SUB:
        blk = 2 * half
        is_right = (row_w & (blk - 1)) >= half
        same_blk = (row ^ col) < blk
        valid = same_blk & ((row & (blk - 1)) >= half) & ((col & (blk - 1)) < half)
        masks.append((half, is_right, valid))
        half //= 2
    return masks


def _chunk_scores(q, k, b, masks):
    c = q.shape[0]
    row8 = lax.broadcasted_iota(jnp.int32, (SUB, HEAD), 0)
    lane = lax.broadcasted_iota(jnp.int32, (SUB, c), 1)
    diag = []
    for i in range(c // SUB):
        bi, qi, ki = (t[SUB * i:SUB * (i + 1)] for t in (b, q, k))
        blk = jnp.zeros((SUB, c), F32)
        for s in range(SUB):
            arg = jnp.where(row8 >= s, bi - bi[s:s + 1], NEG_BIG)
            a = jnp.sum(qi * ki[s:s + 1] * jnp.exp(arg), axis=-1, keepdims=True)
            blk = jnp.where(lane == SUB * i + s, a, blk)
        diag.append(blk)
    sc = jnp.concatenate(diag, axis=0)
    for half, is_right, valid in masks:
        blk = 2 * half
        mid = jnp.concatenate(
            [jnp.broadcast_to(b[s0 + half - 1:s0 + half], (blk, HEAD)) for s0 in range(0, c, blk)],
            axis=0)
        x = (jnp.where(is_right, q, k) * jnp.exp(-jnp.abs(b - mid))).astype(BF16)
        sc = sc + jnp.where(valid, lax.dot_general(x, x, _NT, preferred_element_type=F32), 0.0)
    return sc


def _chunk_state(k, v, b, st):
    b_last = b[b.shape[0] - 1:, :]
    kd = (k * jnp.exp(b_last - b)).astype(BF16)
    return st * jnp.exp(b_last) + lax.dot_general(v.astype(BF16), kd, _TN, preferred_element_type=F32)


def _hgrn_out(o, g, zog):
    return _rms_rows(o, g) * _silu(zog)


def _hgrn_prompt_kernel(zq_ref, zf_ref, zi_ref, zog_ref, mf_ref, mi_ref, lbl_ref, g_ref,
                        o_ref, s_ref):
    lb = _lower_bound(lbl_ref[...])
    g = g_ref[...]
    n_chunks = zq_ref.shape[0] // CHUNK

    mk, mlf = _gates(mf_ref[...], lb)
    st0 = _chunk_state(mk, mi_ref[...], _cumsum_rows(mlf), jnp.zeros((HEAD, HEAD), F32))

    masks = _chunk_masks(CHUNK)

    def body(c, st):
        rows = pl.ds(pl.multiple_of(c * CHUNK, CHUNK), CHUNK)
        q = _silu(zq_ref[rows, :])
        k, lf = _gates(zf_ref[rows, :], lb)
        v = zi_ref[rows, :]
        b = _cumsum_rows(lf)
        o = lax.dot_general((q * jnp.exp(b)).astype(BF16), st.astype(BF16), _NT,
                            preferred_element_type=F32)
        sc = _chunk_scores(q, k, b, masks)
        o = o + jnp.dot(sc.astype(BF16), v.astype(BF16), preferred_element_type=F32)
        o_ref[rows, :] = _hgrn_out(o, g, zog_ref[rows, :]).astype(o_ref.dtype)
        return _chunk_state(k, v, b, st)

    st = lax.fori_loop(0, n_chunks, body, st0)
    s_ref[0, 0] = st.T


def _hgrn_prompt(z, z_small, lb_logits, hgrn_g, batch, seq):
    hb = A_WIDTH // HEAD
    col = lambda base: (lambda b, h: (b, base * hb + h))
    meta_blk = z_small.shape[0] // N_META - 1
    meta = lambda base: (lambda b, h: (meta_blk, base * hb + h))
    blocks = 4 * seq * HEAD * 4 + seq * HEAD * 2 + 2 * N_META * HEAD * 4 + HEAD * HEAD * 4
    return pl.pallas_call(
        _hgrn_prompt_kernel,
        grid=(batch, A_HEADS),
        in_specs=[pl.BlockSpec((seq, HEAD), col(0)),
                  pl.BlockSpec((seq, HEAD), col(1)),
                  pl.BlockSpec((seq, HEAD), col(2)),
                  pl.BlockSpec((seq, HEAD), col(3)),
                  pl.BlockSpec((N_META, HEAD), meta(1)),
                  pl.BlockSpec((N_META, HEAD), meta(2)),
                  pl.BlockSpec((lb_logits.shape[0], HEAD), lambda b, h: (0, h)),
                  pl.BlockSpec((1, HEAD), lambda b, h: (0, h))],
        out_specs=[pl.BlockSpec((seq, HEAD), lambda b, h: (b, h)),
                   pl.BlockSpec((1, 1, HEAD, HEAD), lambda b, h: (b, h, 0, 0))],
        out_shape=[jax.ShapeDtypeStruct((batch * seq, A_WIDTH), BF16),
                   jax.ShapeDtypeStruct((batch, A_HEADS, HEAD, HEAD), F32)],
        compiler_params=_params(_vmem_limit(blocks, 4 * 1024 * 1024), 2),
        name="hgrn_prompt",
    )(z, z, z, z, z_small, z_small, lb_logits, hgrn_g)


SAMPLE_ROWS = 8
SPLAT_ROWS = 16


def _hgrn_sample_kernel(z_ref, s_ref, lbl_ref, g_ref, o_ref, so_ref):
    row = lax.broadcasted_iota(jnp.int32, (SPLAT_ROWS, HEAD), 0)
    ones_f = jnp.where(row < 3, 1.0, 0.0)
    ones_q = jnp.where(row == 4, 1.0, 0.0)

    def per_seq(r, carry):
        zrow = z_ref[r]
        for h in range(A_HEADS):
            sl = lambda base: zrow[:, base * A_WIDTH + h * HEAD: base * A_WIDTH + (h + 1) * HEAD]
            lb = _lower_bound(lbl_ref[:, h * HEAD:(h + 1) * HEAD])
            q = _silu(sl(0))
            fg = lb + (1.0 - lb) * _sigmoid(sl(1))
            k = 1.0 - fg
            v = sl(2)
            f_hi = fg.astype(BF16).astype(F32)
            f_mid = (fg - f_hi).astype(BF16).astype(F32)
            f_lo = fg - f_hi - f_mid
            a = jnp.where(row == 0, f_hi, jnp.where(row == 1, f_mid, jnp.where(
                row == 2, f_lo, jnp.where(row == 3, k, jnp.where(row == 4, q, 0.0)))))
            bmat = jnp.concatenate([ones_f, ones_q, jnp.where(row == 3, v, 0.0)], axis=1)
            gm = lax.dot_general(a.astype(BF16), bmat.astype(BF16), _TN,
                                 preferred_element_type=F32)
            s_new = gm[:, 0:HEAD] * s_ref[r, h] + gm[:, 2 * HEAD:3 * HEAD]
            so_ref[r, h] = s_new
            o = jnp.sum(gm[:, HEAD:2 * HEAD] * s_new, axis=0, keepdims=True)
            og = sl(3)
            o_ref[r, :, h * HEAD:(h + 1) * HEAD] = _hgrn_out(
                o, g_ref[:, h * HEAD:(h + 1) * HEAD], og)
        return carry

    lax.fori_loop(0, SAMPLE_ROWS, per_seq, 0)


def _hgrn_sample(z_small, state, lb_logits, hgrn_g):
    nb = state.shape[0]
    blocks = SAMPLE_ROWS * (IN_COLS * 4 + A_WIDTH * 4 + 2 * A_HEADS * HEAD * HEAD * 4)
    return pl.pallas_call(
        _hgrn_sample_kernel,
        grid=(nb // SAMPLE_ROWS,),
        in_specs=[pl.BlockSpec((SAMPLE_ROWS, 1, IN_COLS), lambda i: (i, 0, 0)),
                  pl.BlockSpec((SAMPLE_ROWS, A_HEADS, HEAD, HEAD), lambda i: (i, 0, 0, 0)),
                  pl.BlockSpec(lb_logits.shape, lambda i: (0, 0)),
                  pl.BlockSpec((1, A_WIDTH), lambda i: (0, 0))],
        out_specs=[pl.BlockSpec((SAMPLE_ROWS, 1, A_WIDTH), lambda i: (i, 0, 0)),
                   pl.BlockSpec((SAMPLE_ROWS, A_HEADS, HEAD, HEAD), lambda i: (i, 0, 0, 0))],
        out_shape=[jax.ShapeDtypeStruct((nb, 1, A_WIDTH), F32),
                   jax.ShapeDtypeStruct(state.shape, F32)],
        compiler_params=_params(_vmem_limit(blocks, 2 * 1024 * 1024), 1),
        name="hgrn_sample",
    )(z_small, state, lb_logits, hgrn_g)


def _group_norm_swish(c, gn_g, gn_b):
    mu = jnp.mean(c, axis=-1, keepdims=True)
    d = c - mu
    var = jnp.mean(d * d, axis=-1, keepdims=True)
    return _silu(d * lax.rsqrt(var + EPS) * gn_g + gn_b)


CONV_PAD = 32
CONV_TILE = 256


def _conv_prompt_kernel(ga_ref, gb_ref, ma_ref, mb_ref, w_ref, cb_ref, gg_ref, gbias_ref,
                        o_ref, buf_ref, u_ref):
    seq = ga_ref.shape[0]
    u_ref[0:CONV_PAD - N_META, :] = jnp.zeros((CONV_PAD - N_META, GROUP), F32)
    u_ref[CONV_PAD - N_META:CONV_PAD, :] = ma_ref[...] * _sigmoid(mb_ref[...])
    u_ref[CONV_PAD:, :] = ga_ref[...] * _sigmoid(gb_ref[...])
    bias = cb_ref[...]
    for t0 in range(0, seq, CONV_TILE):
        acc = jnp.zeros((CONV_TILE, GROUP), F32) + bias
        for j in range(CONV_W):
            start = t0 + CONV_PAD - HIST + j
            acc = acc + w_ref[j:j + 1, :] * u_ref[start:start + CONV_TILE, :]
        o_ref[t0:t0 + CONV_TILE, :] = _group_norm_swish(acc, gg_ref[...], gbias_ref[...]).astype(
            o_ref.dtype)
    buf_ref[0] = u_ref[CONV_PAD + seq - HIST:CONV_PAD + seq, :]


def _conv_prompt(z, z_small, conv_w, conv_b, gn_g, gn_b, batch, seq):
    ga0 = 4 * A_WIDTH // GROUP
    gb0 = ga0 + B_WIDTH // GROUP
    meta_blk = z_small.shape[0] // N_META - 1
    vec = pl.BlockSpec((1, GROUP), lambda b, g: (0, g))
    blocks = 2 * seq * GROUP * 4 + seq * GROUP * 2 + 32 * GROUP * 4 * 4
    return pl.pallas_call(
        _conv_prompt_kernel,
        grid=(batch, B_GROUPS),
        in_specs=[pl.BlockSpec((seq, GROUP), lambda b, g: (b, ga0 + g)),
                  pl.BlockSpec((seq, GROUP), lambda b, g: (b, gb0 + g)),
                  pl.BlockSpec((N_META, GROUP), lambda b, g: (meta_blk, ga0 + g)),
                  pl.BlockSpec((N_META, GROUP), lambda b, g: (meta_blk, gb0 + g)),
                  pl.BlockSpec((CONV_W, GROUP), lambda b, g: (0, g)),
                  vec, vec, vec],
        out_specs=[pl.BlockSpec((seq, GROUP), lambda b, g: (b, g)),
                   pl.BlockSpec((1, HIST, GROUP), lambda b, g: (b, 0, g))],
        out_shape=[jax.ShapeDtypeStruct((batch * seq, B_WIDTH), BF16),
                   jax.ShapeDtypeStruct((batch, HIST, B_WIDTH), F32)],
        scratch_shapes=[pltpu.VMEM((CONV_PAD + seq, GROUP), F32)],
        compiler_params=_params(_vmem_limit(blocks, (CONV_PAD + seq) * GROUP * 4 + 4 * 1024 * 1024), 2),
        name="conv_prompt",
    )(z, z, z_small, z_small, conv_w, conv_b, gn_g, gn_b)


def _conv_sample_kernel(z_ref, st_ref, w_ref, cb_ref, gg_ref, gbias_ref, o_ref, buf_ref):
    w_hist = w_ref[0:HIST, :]
    w_last = w_ref[HIST:CONV_W, :]

    def per_seq(r, carry):
        zrow = z_ref[r]
        u = zrow[:, 4 * A_WIDTH:4 * A_WIDTH + B_WIDTH] * _sigmoid(zrow[:, 4 * A_WIDTH + B_WIDTH:])
        hist = st_ref[r]
        c = jnp.sum(hist * w_hist, axis=0, keepdims=True) + w_last * u + cb_ref[...]
        for g in range(B_GROUPS):
            sl = slice(g * GROUP, (g + 1) * GROUP)
            o_ref[r, :, sl] = _group_norm_swish(c[:, sl], gg_ref[:, sl], gbias_ref[:, sl])
        buf_ref[r, 0:HIST - 1, :] = st_ref[r, 1:HIST, :]
        buf_ref[r, HIST - 1:HIST, :] = u
        return carry

    lax.fori_loop(0, SAMPLE_ROWS, per_seq, 0)


def _conv_sample(z_small, state, conv_w, conv_b, gn_g, gn_b):
    nb = state.shape[0]
    vec = pl.BlockSpec((1, B_WIDTH), lambda i: (0, 0))
    blocks = SAMPLE_ROWS * (IN_COLS * 4 + B_WIDTH * 4 + 2 * 32 * B_WIDTH * 4) + 32 * B_WIDTH * 4
    return pl.pallas_call(
        _conv_sample_kernel,
        grid=(nb // SAMPLE_ROWS,),
        in_specs=[pl.BlockSpec((SAMPLE_ROWS, 1, IN_COLS), lambda i: (i, 0, 0)),
                  pl.BlockSpec((SAMPLE_ROWS, HIST, B_WIDTH), lambda i: (i, 0, 0)),
                  pl.BlockSpec((CONV_W, B_WIDTH), lambda i: (0, 0)),
                  vec, vec, vec],
        out_specs=[pl.BlockSpec((SAMPLE_ROWS, 1, B_WIDTH), lambda i: (i, 0, 0)),
                   pl.BlockSpec((SAMPLE_ROWS, HIST, B_WIDTH), lambda i: (i, 0, 0))],
        out_shape=[jax.ShapeDtypeStruct((nb, 1, B_WIDTH), F32),
                   jax.ShapeDtypeStruct(state.shape, F32)],
        compiler_params=_params(_vmem_limit(blocks, 2 * 1024 * 1024), 1),
        name="conv_sample",
    )(z_small, state, conv_w, conv_b, gn_g, gn_b)


def _outproj_kernel(oa_ref, ob_ref, x_ref, wa_ref, wb_ref, g_ref, h_ref, hf_ref):
    h = x_ref[...]
    h = h + jnp.dot(oa_ref[...].astype(BF16), wa_ref[...], preferred_element_type=F32)
    h = h + jnp.dot(ob_ref[...].astype(BF16), wb_ref[...], preferred_element_type=F32)
    h_ref[...] = h
    hf_ref[...] = _rms_rows(h, g_ref[...]).astype(BF16)


def _outproj(oa, ob, x, w_bf16, g, tm):
    m = x.shape[0]
    act = oa.dtype.itemsize
    blocks = 2 * tm * A_WIDTH * act + tm * D_MODEL * (4 + 4 + 2) + D_MODEL * D_MODEL * 2
    return pl.pallas_call(
        _outproj_kernel,
        grid=(m // tm,),
        in_specs=[pl.BlockSpec((tm, A_WIDTH), lambda i: (i, 0)),
                  pl.BlockSpec((tm, B_WIDTH), lambda i: (i, 0)),
                  pl.BlockSpec((tm, D_MODEL), lambda i: (i, 0)),
                  pl.BlockSpec((A_WIDTH, D_MODEL), lambda i: (0, 0)),
                  pl.BlockSpec((B_WIDTH, D_MODEL), lambda i: (1, 0)),
                  pl.BlockSpec((1, D_MODEL), lambda i: (0, 0))],
        out_specs=[pl.BlockSpec((tm, D_MODEL), lambda i: (i, 0)),
                   pl.BlockSpec((tm, D_MODEL), lambda i: (i, 0))],
        out_shape=[jax.ShapeDtypeStruct((m, D_MODEL), F32),
                   jax.ShapeDtypeStruct((m, D_MODEL), BF16)],
        compiler_params=_params(_vmem_limit(blocks, 2 * tm * D_MODEL * 4), 1),
        name="outproj",
    )(oa, ob, x, w_bf16, w_bf16, g)


def _ffn_kernel(hf_ref, h_ref, wg_ref, wu_ref, wd_ref, g_ref, y_ref, acc_ref):
    f = pl.program_id(1)

    @pl.when(f == 0)
    def _():
        acc_ref[...] = h_ref[...]

    hf = hf_ref[...]
    gate = jnp.dot(hf, wg_ref[...], preferred_element_type=F32)
    up = jnp.dot(hf, wu_ref[...], preferred_element_type=F32)
    act = (_silu(gate) * up).astype(BF16)
    acc_ref[...] += jnp.dot(act, wd_ref[...], preferred_element_type=F32)

    @pl.when(f == pl.num_programs(1) - 1)
    def _():
        y_ref[...] = _rms_rows(acc_ref[...], g_ref[...])


def _ffn(hf, h1, wg, wu, wd, g, tm, tf):
    m = hf.shape[0]
    blocks = tm * D_MODEL * (2 + 4 + 4) + 3 * D_MODEL * tf * 2
    resident = tm * D_MODEL * 4 + 3 * tm * tf * 4
    return pl.pallas_call(
        _ffn_kernel,
        grid=(m // tm, D_FF // tf),
        in_specs=[pl.BlockSpec((tm, D_MODEL), lambda i, f: (i, 0)),
                  pl.BlockSpec((tm, D_MODEL), lambda i, f: (i, 0)),
                  pl.BlockSpec((D_MODEL, tf), lambda i, f: (0, f)),
                  pl.BlockSpec((D_MODEL, tf), lambda i, f: (0, f)),
                  pl.BlockSpec((tf, D_MODEL), lambda i, f: (f, 0)),
                  pl.BlockSpec((1, D_MODEL), lambda i, f: (0, 0))],
        out_specs=pl.BlockSpec((tm, D_MODEL), lambda i, f: (i, 0)),
        out_shape=jax.ShapeDtypeStruct((m, D_MODEL), F32),
        scratch_shapes=[pltpu.VMEM((tm, D_MODEL), F32)],
        compiler_params=_params(_vmem_limit(blocks, resident), 2),
        name="ffn",
    )(hf, h1, wg, wu, wd, g)


def kernel(x_prompt, x_sample, state_hgrn, state_conv, meta_tokens, norm_mix_g, w_in, lb_logits,
           hgrn_norm_g, conv_w, conv_b, gn_g, gn_b, w_out, norm_ffn_g, w_ffn_gate, w_ffn_up,
           w_ffn_down, norm_final_g):
    batch, seq, _ = x_prompt.shape
    nb = x_sample.shape[0]
    assert x_sample.shape[1] == 1 and norm_mix_g.shape[0] == 1
    assert seq % CHUNK == 0 and seq % CONV_TILE == 0 and nb % N_META == 0

    w_in_b = w_in[0].astype(BF16)
    w_out_b = w_out[0].astype(BF16)
    wg_b = w_ffn_gate[0].astype(BF16)
    wu_b = w_ffn_up[0].astype(BF16)
    wd_b = w_ffn_down[0].astype(BF16)
    g_final = norm_final_g[None, :]

    xp = x_prompt.reshape(batch * seq, D_MODEL)
    xs = x_sample.reshape(nb, D_MODEL)
    x_small = jnp.concatenate([xs, meta_tokens], axis=0)

    z = _inproj(xp, norm_mix_g, w_in_b, tm=1024, tn=1024)
    z_small = _inproj(x_small, norm_mix_g, w_in_b, tm=x_small.shape[0], tn=1024)

    oa_p, s_p = _hgrn_prompt(z, z_small, lb_logits, hgrn_norm_g, batch, seq)
    ob_p, c_p = _conv_prompt(z, z_small, conv_w[0], conv_b, gn_g, gn_b, batch, seq)
    z_s = z_small[:nb].reshape(nb, 1, IN_COLS)
    oa_s, s_s = _hgrn_sample(z_s, state_hgrn[0], lb_logits, hgrn_norm_g)
    ob_s, c_s = _conv_sample(z_s, state_conv[0], conv_w[0], conv_b, gn_g, gn_b)
    oa_s = oa_s.reshape(nb, A_WIDTH)
    ob_s = ob_s.reshape(nb, B_WIDTH)

    h_p, hf_p = _outproj(oa_p, ob_p, xp, w_out_b, norm_ffn_g, tm=256)
    h_s, hf_s = _outproj(oa_s, ob_s, xs, w_out_b, norm_ffn_g, tm=nb)

    y_p = _ffn(hf_p, h_p, wg_b, wu_b, wd_b, g_final, tm=512, tf=512)
    y_s = _ffn(hf_s, h_s, wg_b, wu_b, wd_b, g_final, tm=nb, tf=512)

    return (y_p.reshape(batch, seq, D_MODEL), y_s.reshape(nb, 1, D_MODEL),
            s_p[None], c_p[None], s_s[None], c_s[None])
```

```python
import functools

import jax
import jax.numpy as jnp
from jax import lax
from jax.experimental import pallas as pl
from jax.experimental.pallas import tpu as pltpu

F32 = jnp.float32
BF16 = jnp.bfloat16

D_MODEL = 2048
N_META = 16
A_WIDTH = 1024
B_WIDTH = 1024
HEAD = 128
A_HEADS = A_WIDTH // HEAD
CONV_W = 31
HIST = CONV_W - 1
B_GROUPS = 8
GROUP = B_WIDTH // B_GROUPS
D_FF = 5632
IN_COLS = 4 * A_WIDTH + 2 * B_WIDTH
EPS = 1e-6

V7X_LANES = 128
V7X_SUBLANES = 8
V7X_VMEM_BYTES = 64 * 1024 * 1024
V7X_VMEM_USABLE = V7X_VMEM_BYTES - 8 * 1024 * 1024

CHUNK = 64
SUB = V7X_SUBLANES
NEG_BIG = -1e30
HEADS_PER_STEP = 4

_NT = (((1,), (1,)), ((), ()))
_TN = (((0,), (0,)), ((), ()))


def _vmem_limit(pipelined_block_bytes, resident_bytes):
    est = 2 * pipelined_block_bytes + resident_bytes
    return int(min(max(est, 16 * 1024 * 1024), V7X_VMEM_USABLE))


def _params(vmem_bytes, ndims):
    return pltpu.CompilerParams(dimension_semantics=("arbitrary",) * ndims,
                                vmem_limit_bytes=vmem_bytes)


def _sigmoid(x):
    return 1.0 / (1.0 + jnp.exp(-x))


def _silu(x):
    return x * _sigmoid(x)


def _rms_rows(x, g):
    return x * lax.rsqrt(jnp.mean(x * x, axis=-1, keepdims=True) + EPS) * g


def _inproj_kernel(x_ref, g_ref, w_ref, z_ref, xn_ref):
    @pl.when(pl.program_id(1) == 0)
    def _():
        xn_ref[...] = _rms_rows(x_ref[...], g_ref[...]).astype(BF16)

    z_ref[...] = jnp.dot(xn_ref[...], w_ref[...], preferred_element_type=F32)


def _inproj(x, g, w_bf16, tm, tn):
    m = x.shape[0]
    n = w_bf16.shape[1]
    blocks = tm * D_MODEL * 4 + D_MODEL * tn * 2 + tm * tn * 4
    resident = tm * D_MODEL * 2 + tm * tn * 4
    return pl.pallas_call(
        _inproj_kernel,
        grid=(m // tm, n // tn),
        in_specs=[pl.BlockSpec((tm, D_MODEL), lambda i, j: (i, 0)),
                  pl.BlockSpec((1, D_MODEL), lambda i, j: (0, 0)),
                  pl.BlockSpec((D_MODEL, tn), lambda i, j: (0, j))],
        out_specs=pl.BlockSpec((tm, tn), lambda i, j: (i, j)),
        out_shape=jax.ShapeDtypeStruct((m, n), F32),
        scratch_shapes=[pltpu.VMEM((tm, D_MODEL), BF16)],
        compiler_params=_params(_vmem_limit(blocks, resident), 2),
        name="inproj",
    )(x, g, w_bf16)


def _lower_bound(lb_logits):
    m = jnp.max(lb_logits, axis=0, keepdims=True)
    e = jnp.exp(lb_logits - m)
    return e[0:1, :] / jnp.sum(e, axis=0, keepdims=True)


def _cumsum_rows(x):
    n = x.shape[0]
    row = lax.broadcasted_iota(jnp.int32, x.shape, 0)
    s = 1
    while s < n:
        x = x + jnp.where(row >= s, pltpu.roll(x, s, 0), 0.0)
        s *= 2
    return x


def _gates(zf, lb):
    fg = lb + (1.0 - lb) * _sigmoid(zf)
    return 1.0 - fg, jnp.log(fg)


def _chunk_masks(c):
    row_w = lax.broadcasted_iota(jnp.int32, (c, HEAD), 0)
    row = lax.broadcasted_iota(jnp.int32, (c, c), 0)
    col = lax.broadcasted_iota(jnp.int32, (c, c), 1)
    masks = []
    half = c // 2
    while half >= SUB:
        blk = 2 * half
        is_right = (row_w & (blk - 1)) >= half
        same_blk = (row ^ col) < blk
        valid = same_blk & ((row & (blk - 1)) >= half) & ((col & (blk - 1)) < half)
        masks.append((half, is_right, valid))
        half //= 2
    return masks


def _chunk_scores(q, k, b, masks):
    c = q.shape[0]
    row8 = lax.broadcasted_iota(jnp.int32, (SUB, HEAD), 0)
    lane = lax.broadcasted_iota(jnp.int32, (SUB, c), 1)
    diag = []
    for i in range(c // SUB):
        bi, qi, ki = (t[SUB * i:SUB * (i + 1)] for t in (b, q, k))
        blk = jnp.zeros((SUB, c), F32)
        for s in range(SUB):
            arg = jnp.where(row8 >= s, bi - bi[s:s + 1], NEG_BIG)
            a = jnp.sum(qi * ki[s:s + 1] * jnp.exp(arg), axis=-1, keepdims=True)
            blk = jnp.where(lane == SUB * i + s, a, blk)
        diag.append(blk)
    sc = jnp.concatenate(diag, axis=0)
    for half, is_right, valid in masks:
        blk = 2 * half
        mid = jnp.concatenate(
            [jnp.broadcast_to(b[s0 + half - 1:s0 + half], (blk, HEAD)) for s0 in range(0, c, blk)],
            axis=0)
        x = (jnp.where(is_right, q, k) * jnp.exp(-jnp.abs(b - mid))).astype(BF16)
        sc = sc + jnp.where(valid, lax.dot_general(x, x, _NT, preferred_element_type=F32), 0.0)
    return sc


def _chunk_state(k, v, b, st):
    b_last = b[b.shape[0] - 1:, :]
    kd = (k * jnp.exp(b_last - b)).astype(BF16)
    return st * jnp.exp(b_last) + lax.dot_general(v.astype(BF16), kd, _TN, preferred_element_type=F32)


def _hgrn_out(o, g, zog):
    return _rms_rows(o, g) * _silu(zog)


def _hgrn_prompt_kernel(zq_ref, zf_ref, zi_ref, zog_ref, mf_ref, mi_ref, lbl_ref, g_ref,
                        o_ref, s_ref):
    n_chunks = zq_ref.shape[0] // CHUNK
    heads = [slice(h * HEAD, (h + 1) * HEAD) for h in range(HEADS_PER_STEP)]
    lbs = [_lower_bound(lbl_ref[:, hs]) for hs in heads]

    def meta_state(hs, lb):
        mk, mlf = _gates(mf_ref[:, hs], lb)
        return _chunk_state(mk, mi_ref[:, hs], _cumsum_rows(mlf), jnp.zeros((HEAD, HEAD), F32))

    masks = _chunk_masks(CHUNK)

    def body(c, sts):
        rows = pl.ds(pl.multiple_of(c * CHUNK, CHUNK), CHUNK)
        new = []
        for hs, lb, st in zip(heads, lbs, sts):
            q = _silu(zq_ref[rows, hs])
            k, lf = _gates(zf_ref[rows, hs], lb)
            v = zi_ref[rows, hs]
            b = _cumsum_rows(lf)
            o = lax.dot_general((q * jnp.exp(b)).astype(BF16), st.astype(BF16), _NT,
                                preferred_element_type=F32)
            sc = _chunk_scores(q, k, b, masks)
            o = o + jnp.dot(sc.astype(BF16), v.astype(BF16), preferred_element_type=F32)
            o_ref[rows, hs] = _hgrn_out(o, g_ref[:, hs], zog_ref[rows, hs]).astype(o_ref.dtype)
            new.append(_chunk_state(k, v, b, st))
        return tuple(new)

    sts = lax.fori_loop(0, n_chunks, body, tuple(meta_state(hs, lb) for hs, lb in zip(heads, lbs)))
    for h, st in enumerate(sts):
        s_ref[0, h] = st.T


def _hgrn_prompt(z, z_small, lb_logits, hgrn_g, batch, seq):
    width = HEADS_PER_STEP * HEAD
    hb = A_WIDTH // width
    col = lambda base: (lambda b, h: (b, base * hb + h))
    meta_blk = z_small.shape[0] // N_META - 1
    meta = lambda base: (lambda b, h: (meta_blk, base * hb + h))
    blocks = 4 * seq * width * 4 + seq * width * 2 + 2 * N_META * width * 4 + width * HEAD * 4
    return pl.pallas_call(
        _hgrn_prompt_kernel,
        grid=(batch, hb),
        in_specs=[pl.BlockSpec((seq, width), col(0)),
                  pl.BlockSpec((seq, width), col(1)),
                  pl.BlockSpec((seq, width), col(2)),
                  pl.BlockSpec((seq, width), col(3)),
                  pl.BlockSpec((N_META, width), meta(1)),
                  pl.BlockSpec((N_META, width), meta(2)),
                  pl.BlockSpec((lb_logits.shape[0], width), lambda b, h: (0, h)),
                  pl.BlockSpec((1, width), lambda b, h: (0, h))],
        out_specs=[pl.BlockSpec((seq, width), lambda b, h: (b, h)),
                   pl.BlockSpec((1, HEADS_PER_STEP, HEAD, HEAD), lambda b, h: (b, h, 0, 0))],
        out_shape=[jax.ShapeDtypeStruct((batch * seq, A_WIDTH), BF16),
                   jax.ShapeDtypeStruct((batch, A_HEADS, HEAD, HEAD), F32)],
        compiler_params=_params(_vmem_limit(blocks, 4 * 1024 * 1024), 2),
        name="hgrn_prompt",
    )(z, z, z, z, z_small, z_small, lb_logits, hgrn_g)


SAMPLE_ROWS = 8
SPLAT_ROWS = 16


def _hgrn_sample_kernel(z_ref, s_ref, lbl_ref, g_ref, o_ref, so_ref):
    row = lax.broadcasted_iota(jnp.int32, (SPLAT_ROWS, HEAD), 0)
    ones_f = jnp.where(row < 3, 1.0, 0.0)
    ones_q = jnp.where(row == 4, 1.0, 0.0)

    def per_seq(r, carry):
        zrow = z_ref[r]
        for h in range(A_HEADS):
            sl = lambda base: zrow[:, base * A_WIDTH + h * HEAD: base * A_WIDTH + (h + 1) * HEAD]
            lb = _lower_bound(lbl_ref[:, h * HEAD:(h + 1) * HEAD])
            q = _silu(sl(0))
            fg = lb + (1.0 - lb) * _sigmoid(sl(1))
            k = 1.0 - fg
            v = sl(2)
            f_hi = fg.astype(BF16).astype(F32)
            f_mid = (fg - f_hi).astype(BF16).astype(F32)
            f_lo = fg - f_hi - f_mid
            a = jnp.where(row == 0, f_hi, jnp.where(row == 1, f_mid, jnp.where(
                row == 2, f_lo, jnp.where(row == 3, k, jnp.where(row == 4, q, 0.0)))))
            bmat = jnp.concatenate([ones_f, ones_q, jnp.where(row == 3, v, 0.0)], axis=1)
            gm = lax.dot_general(a.astype(BF16), bmat.astype(BF16), _TN,
                                 preferred_element_type=F32)
            s_new = gm[:, 0:HEAD] * s_ref[r, h] + gm[:, 2 * HEAD:3 * HEAD]
            so_ref[r, h] = s_new
            o = jnp.sum(gm[:, HEAD:2 * HEAD] * s_new, axis=0, keepdims=True)
            og = sl(3)
            o_ref[r, :, h * HEAD:(h + 1) * HEAD] = _hgrn_out(
                o, g_ref[:, h * HEAD:(h + 1) * HEAD], og)
        return carry

    lax.fori_loop(0, SAMPLE_ROWS, per_seq, 0)


def _hgrn_sample(z_small, state, lb_logits, hgrn_g):
    nb = state.shape[0]
    blocks = SAMPLE_ROWS * (IN_COLS * 4 + A_WIDTH * 4 + 2 * A_HEADS * HEAD * HEAD * 4)
    return pl.pallas_call(
        _hgrn_sample_kernel,
        grid=(nb // SAMPLE_ROWS,),
        in_specs=[pl.BlockSpec((SAMPLE_ROWS, 1, IN_COLS), lambda i: (i, 0, 0)),
                  pl.BlockSpec((SAMPLE_ROWS, A_HEADS, HEAD, HEAD), lambda i: (i, 0, 0, 0)),
                  pl.BlockSpec(lb_logits.shape, lambda i: (0, 0)),
                  pl.BlockSpec((1, A_WIDTH), lambda i: (0, 0))],
        out_specs=[pl.BlockSpec((SAMPLE_ROWS, 1, A_WIDTH), lambda i: (i, 0, 0)),
                   pl.BlockSpec((SAMPLE_ROWS, A_HEADS, HEAD, HEAD), lambda i: (i, 0, 0, 0))],
        out_shape=[jax.ShapeDtypeStruct((nb, 1, A_WIDTH), F32),
                   jax.ShapeDtypeStruct(state.shape, F32)],
        compiler_params=_params(_vmem_limit(blocks, 2 * 1024 * 1024), 1),
        name="hgrn_sample",
    )(z_small, state, lb_logits, hgrn_g)


def _group_norm_swish(c, gn_g, gn_b):
    mu = jnp.mean(c, axis=-1, keepdims=True)
    d = c - mu
    var = jnp.mean(d * d, axis=-1, keepdims=True)
    return _silu(d * lax.rsqrt(var + EPS) * gn_g + gn_b)


CONV_PAD = 32
CONV_TILE = 256


def _conv_prompt_kernel(ga_ref, gb_ref, ma_ref, mb_ref, w_ref, cb_ref, gg_ref, gbias_ref,
                        o_ref, buf_ref, u_ref):
    seq = ga_ref.shape[0]
    u_ref[0:CONV_PAD - N_META, :] = jnp.zeros((CONV_PAD - N_META, GROUP), F32)
    u_ref[CONV_PAD - N_META:CONV_PAD, :] = ma_ref[...] * _sigmoid(mb_ref[...])
    u_ref[CONV_PAD:, :] = ga_ref[...] * _sigmoid(gb_ref[...])
    bias = cb_ref[...]
    for t0 in range(0, seq, CONV_TILE):
        acc = jnp.zeros((CONV_TILE, GROUP), F32) + bias
        for j in range(CONV_W):
            start = t0 + CONV_PAD - HIST + j
            acc = acc + w_ref[j:j + 1, :] * u_ref[start:start + CONV_TILE, :]
        o_ref[t0:t0 + CONV_TILE, :] = _group_norm_swish(acc, gg_ref[...], gbias_ref[...]).astype(
            o_ref.dtype)
    buf_ref[0] = u_ref[CONV_PAD + seq - HIST:CONV_PAD + seq, :]


def _conv_prompt(z, z_small, conv_w, conv_b, gn_g, gn_b, batch, seq):
    ga0 = 4 * A_WIDTH // GROUP
    gb0 = ga0 + B_WIDTH // GROUP
    meta_blk = z_small.shape[0] // N_META - 1
    vec = pl.BlockSpec((1, GROUP), lambda b, g: (0, g))
    blocks = 2 * seq * GROUP * 4 + seq * GROUP * 2 + 32 * GROUP * 4 * 4
    return pl.pallas_call(
        _conv_prompt_kernel,
        grid=(batch, B_GROUPS),
        in_specs=[pl.BlockSpec((seq, GROUP), lambda b, g: (b, ga0 + g)),
                  pl.BlockSpec((seq, GROUP), lambda b, g: (b, gb0 + g)),
                  pl.BlockSpec((N_META, GROUP), lambda b, g: (meta_blk, ga0 + g)),
                  pl.BlockSpec((N_META, GROUP), lambda b, g: (meta_blk, gb0 + g)),
                  pl.BlockSpec((CONV_W, GROUP), lambda b, g: (0, g)),
                  vec, vec, vec],
        out_specs=[pl.BlockSpec((seq, GROUP), lambda b, g: (b, g)),
                   pl.BlockSpec((1, HIST, GROUP), lambda b, g: (b, 0, g))],
        out_shape=[jax.ShapeDtypeStruct((batch * seq, B_WIDTH), BF16),
                   jax.ShapeDtypeStruct((batch, HIST, B_WIDTH), F32)],
        scratch_shapes=[pltpu.VMEM((CONV_PAD + seq, GROUP), F32)],
        compiler_params=_params(_vmem_limit(blocks, (CONV_PAD + seq) * GROUP * 4 + 4 * 1024 * 1024), 2),
        name="conv_prompt",
    )(z, z, z_small, z_small, conv_w, conv_b, gn_g, gn_b)


def _conv_sample_kernel(z_ref, st_ref, w_ref, cb_ref, gg_ref, gbias_ref, o_ref, buf_ref):
    w_hist = w_ref[0:HIST, :]
    w_last = w_ref[HIST:CONV_W, :]

    def per_seq(r, carry):
        zrow = z_ref[r]
        u = zrow[:, 4 * A_WIDTH:4 * A_WIDTH + B_WIDTH] * _sigmoid(zrow[:, 4 * A_WIDTH + B_WIDTH:])
        hist = st_ref[r]
        c = jnp.sum(hist * w_hist, axis=0, keepdims=True) + w_last * u + cb_ref[...]
        for g in range(B_GROUPS):
            sl = slice(g * GROUP, (g + 1) * GROUP)
            o_ref[r, :, sl] = _group_norm_swish(c[:, sl], gg_ref[:, sl], gbias_ref[:, sl])
        buf_ref[r, 0:HIST - 1, :] = st_ref[r, 1:HIST, :]
        buf_ref[r, HIST - 1:HIST, :] = u
        return carry

    lax.fori_loop(0, SAMPLE_ROWS, per_seq, 0)


def _conv_sample(z_small, state, conv_w, conv_b, gn_g, gn_b):
    nb = state.shape[0]
    vec = pl.BlockSpec((1, B_WIDTH), lambda i: (0, 0))
    blocks = SAMPLE_ROWS * (IN_COLS * 4 + B_WIDTH * 4 + 2 * 32 * B_WIDTH * 4) + 32 * B_WIDTH * 4
    return pl.pallas_call(
        _conv_sample_kernel,
        grid=(nb // SAMPLE_ROWS,),
        in_specs=[pl.BlockSpec((SAMPLE_ROWS, 1, IN_COLS), lambda i: (i, 0, 0)),
                  pl.BlockSpec((SAMPLE_ROWS, HIST, B_WIDTH), lambda i: (i, 0, 0)),
                  pl.BlockSpec((CONV_W, B_WIDTH), lambda i: (0, 0)),
                  vec, vec, vec],
        out_specs=[pl.BlockSpec((SAMPLE_ROWS, 1, B_WIDTH), lambda i: (i, 0, 0)),
                   pl.BlockSpec((SAMPLE_ROWS, HIST, B_WIDTH), lambda i: (i, 0, 0))],
        out_shape=[jax.ShapeDtypeStruct((nb, 1, B_WIDTH), F32),
                   jax.ShapeDtypeStruct(state.shape, F32)],
        compiler_params=_params(_vmem_limit(blocks, 2 * 1024 * 1024), 1),
        name="conv_sample",
    )(z_small, state, conv_w, conv_b, gn_g, gn_b)


def _outproj_kernel(oa_ref, ob_ref, x_ref, wa_ref, wb_ref, g_ref, h_ref, hf_ref):
    h = x_ref[...]
    h = h + jnp.dot(oa_ref[...].astype(BF16), wa_ref[...], preferred_element_type=F32)
    h = h + jnp.dot(ob_ref[...].astype(BF16), wb_ref[...], preferred_element_type=F32)
    h_ref[...] = h
    hf_ref[...] = _rms_rows(h, g_ref[...]).astype(BF16)


def _outproj(oa, ob, x, w_bf16, g, tm):
    m = x.shape[0]
    act = oa.dtype.itemsize
    blocks = 2 * tm * A_WIDTH * act + tm * D_MODEL * (4 + 4 + 2) + D_MODEL * D_MODEL * 2
    return pl.pallas_call(
        _outproj_kernel,
        grid=(m // tm,),
        in_specs=[pl.BlockSpec((tm, A_WIDTH), lambda i: (i, 0)),
                  pl.BlockSpec((tm, B_WIDTH), lambda i: (i, 0)),
                  pl.BlockSpec((tm, D_MODEL), lambda i: (i, 0)),
                  pl.BlockSpec((A_WIDTH, D_MODEL), lambda i: (0, 0)),
                  pl.BlockSpec((B_WIDTH, D_MODEL), lambda i: (1, 0)),
                  pl.BlockSpec((1, D_MODEL), lambda i: (0, 0))],
        out_specs=[pl.BlockSpec((tm, D_MODEL), lambda i: (i, 0)),
                   pl.BlockSpec((tm, D_MODEL), lambda i: (i, 0))],
        out_shape=[jax.ShapeDtypeStruct((m, D_MODEL), F32),
                   jax.ShapeDtypeStruct((m, D_MODEL), BF16)],
        compiler_params=_params(_vmem_limit(blocks, 2 * tm * D_MODEL * 4), 1),
        name="outproj",
    )(oa, ob, x, w_bf16, w_bf16, g)


def _ffn_kernel(hf_ref, h_ref, wg_ref, wu_ref, wd_ref, g_ref, y_ref, acc_ref):
    f = pl.program_id(1)

    @pl.when(f == 0)
    def _():
        acc_ref[...] = h_ref[...]

    hf = hf_ref[...]
    gate = jnp.dot(hf, wg_ref[...], preferred_element_type=F32)
    up = jnp.dot(hf, wu_ref[...], preferred_element_type=F32)
    act = (_silu(gate) * up).astype(BF16)
    acc_ref[...] += jnp.dot(act, wd_ref[...], preferred_element_type=F32)

    @pl.when(f == pl.num_programs(1) - 1)
    def _():
        y_ref[...] = _rms_rows(acc_ref[...], g_ref[...])


def _ffn(hf, h1, wg, wu, wd, g, tm, tf):
    m = hf.shape[0]
    blocks = tm * D_MODEL * (2 + 4 + 4) + 3 * D_MODEL * tf * 2
    resident = tm * D_MODEL * 4 + 3 * tm * tf * 4
    return pl.pallas_call(
        _ffn_kernel,
        grid=(m // tm, D_FF // tf),
        in_specs=[pl.BlockSpec((tm, D_MODEL), lambda i, f: (i, 0)),
                  pl.BlockSpec((tm, D_MODEL), lambda i, f: (i, 0)),
                  pl.BlockSpec((D_MODEL, tf), lambda i, f: (0, f)),
                  pl.BlockSpec((D_MODEL, tf), lambda i, f: (0, f)),
                  pl.BlockSpec((tf, D_MODEL), lambda i, f: (f, 0)),
                  pl.BlockSpec((1, D_MODEL), lambda i, f: (0, 0))],
        out_specs=pl.BlockSpec((tm, D_MODEL), lambda i, f: (i, 0)),
        out_shape=jax.ShapeDtypeStruct((m, D_MODEL), F32),
        scratch_shapes=[pltpu.VMEM((tm, D_MODEL), F32)],
        compiler_params=_params(_vmem_limit(blocks, resident), 2),
        name="ffn",
    )(hf, h1, wg, wu, wd, g)


def kernel(x_prompt, x_sample, state_hgrn, state_conv, meta_tokens, norm_mix_g, w_in, lb_logits,
           hgrn_norm_g, conv_w, conv_b, gn_g, gn_b, w_out, norm_ffn_g, w_ffn_gate, w_ffn_up,
           w_ffn_down, norm_final_g):
    batch, seq, _ = x_prompt.shape
    nb = x_sample.shape[0]
    assert x_sample.shape[1] == 1 and norm_mix_g.shape[0] == 1
    assert seq % CHUNK == 0 and seq % CONV_TILE == 0 and nb % N_META == 0

    w_in_b = w_in[0].astype(BF16)
    w_out_b = w_out[0].astype(BF16)
    wg_b = w_ffn_gate[0].astype(BF16)
    wu_b = w_ffn_up[0].astype(BF16)
    wd_b = w_ffn_down[0].astype(BF16)
    g_final = norm_final_g[None, :]

    xp = x_prompt.reshape(batch * seq, D_MODEL)
    xs = x_sample.reshape(nb, D_MODEL)
    x_small = jnp.concatenate([xs, meta_tokens], axis=0)

    z = _inproj(xp, norm_mix_g, w_in_b, tm=1024, tn=1024)
    z_small = _inproj(x_small, norm_mix_g, w_in_b, tm=x_small.shape[0], tn=1024)

    oa_p, s_p = _hgrn_prompt(z, z_small, lb_logits, hgrn_norm_g, batch, seq)
    ob_p, c_p = _conv_prompt(z, z_small, conv_w[0], conv_b, gn_g, gn_b, batch, seq)
    z_s = z_small[:nb].reshape(nb, 1, IN_COLS)
    oa_s, s_s = _hgrn_sample(z_s, state_hgrn[0], lb_logits, hgrn_norm_g)
    ob_s, c_s = _conv_sample(z_s, state_conv[0], conv_w[0], conv_b, gn_g, gn_b)
    oa_s = oa_s.reshape(nb, A_WIDTH)
    ob_s = ob_s.reshape(nb, B_WIDTH)

    h_p, hf_p = _outproj(oa_p, ob_p, xp, w_out_b, norm_ffn_g, tm=256)
    h_s, hf_s = _outproj(oa_s, ob_s, xs, w_out_b, norm_ffn_g, tm=nb)

    y_p = _ffn(hf_p, h_p, wg_b, wu_b, wd_b, g_final, tm=512, tf=512)
    y_s = _ffn(hf_s, h_s, wg_b, wu_b, wd_b, g_final, tm=nb, tf=512)

    return (y_p.reshape(batch, seq, D_MODEL), y_s.reshape(nb, 1, D_MODEL),
            s_p[None], c_p[None], s_s[None], c_s[None])
```

```python
import jax
import jax.numpy as jnp
from jax import lax
from jax.experimental import pallas as pl
from jax.experimental.pallas import tpu as pltpu

F32 = jnp.float32
BF16 = jnp.bfloat16

D_MODEL = 2048
N_META = 16
A_WIDTH = 1024
B_WIDTH = 1024
HEAD = 128
A_HEADS = A_WIDTH // HEAD
CONV_W = 31
HIST = CONV_W - 1
B_GROUPS = 8
GROUP = B_WIDTH // B_GROUPS
D_FF = 5632
IN_COLS = 4 * A_WIDTH + 2 * B_WIDTH
EPS = 1e-6

V7X_LANES = 128
V7X_SUBLANES = 8
V7X_VMEM_BYTES = 64 * 1024 * 1024
V7X_VMEM_USABLE = V7X_VMEM_BYTES - 8 * 1024 * 1024

CHUNK = 64
SUB = V7X_SUBLANES
NEG_BIG = -1e30
FAST_BLOCK = 16
SAFE_EXP = 60.0
HEADS_PER_STEP = 4
CHUNK_UNROLL = 2

_NT = (((1,), (1,)), ((), ()))
_TN = (((0,), (0,)), ((), ()))


def _vmem_limit(pipelined_block_bytes, resident_bytes):
    est = 2 * pipelined_block_bytes + resident_bytes
    return int(min(max(est, 16 * 1024 * 1024), V7X_VMEM_USABLE))


def _params(vmem_bytes, ndims):
    return pltpu.CompilerParams(dimension_semantics=("arbitrary",) * ndims,
                                vmem_limit_bytes=vmem_bytes)


def _sigmoid(x):
    return 1.0 / (1.0 + jnp.exp(-x))


def _silu(x):
    return x * _sigmoid(x)


def _rms_rows(x, g):
    return x * lax.rsqrt(jnp.mean(x * x, axis=-1, keepdims=True) + EPS) * g


def _inproj_kernel(x_ref, g_ref, w_ref, z_ref, xn_ref):
    @pl.when(pl.program_id(1) == 0)
    def _():
        xn_ref[...] = _rms_rows(x_ref[...], g_ref[...]).astype(BF16)

    z_ref[...] = jnp.dot(xn_ref[...], w_ref[...], preferred_element_type=F32)


def _inproj(x, g, w_bf16, tm, tn):
    m = x.shape[0]
    n = w_bf16.shape[1]
    blocks = tm * D_MODEL * 4 + D_MODEL * tn * 2 + tm * tn * 4
    resident = tm * D_MODEL * 2 + tm * tn * 4
    return pl.pallas_call(
        _inproj_kernel,
        grid=(m // tm, n // tn),
        in_specs=[pl.BlockSpec((tm, D_MODEL), lambda i, j: (i, 0)),
                  pl.BlockSpec((1, D_MODEL), lambda i, j: (0, 0)),
                  pl.BlockSpec((D_MODEL, tn), lambda i, j: (0, j))],
        out_specs=pl.BlockSpec((tm, tn), lambda i, j: (i, j)),
        out_shape=jax.ShapeDtypeStruct((m, n), F32),
        scratch_shapes=[pltpu.VMEM((tm, D_MODEL), BF16)],
        compiler_params=_params(_vmem_limit(blocks, resident), 2),
        name="inproj",
    )(x, g, w_bf16)


def _lower_bound(lb_logits):
    m = jnp.max(lb_logits, axis=0, keepdims=True)
    e = jnp.exp(lb_logits - m)
    return e[0:1, :] / jnp.sum(e, axis=0, keepdims=True)


def _cumsum_rows(x):
    n = x.shape[0]
    row = lax.broadcasted_iota(jnp.int32, x.shape, 0)
    s = 1
    while s < n:
        x = x + jnp.where(row >= s, pltpu.roll(x, s, 0), 0.0)
        s *= 2
    return x


def _gates(zf, lb):
    fg = lb + (1.0 - lb) * _sigmoid(zf)
    return 1.0 - fg, jnp.log(fg)


def _chunk_masks(c, diag_block):
    row_w = lax.broadcasted_iota(jnp.int32, (c, HEAD), 0)
    row = lax.broadcasted_iota(jnp.int32, (c, c), 0)
    col = lax.broadcasted_iota(jnp.int32, (c, c), 1)
    levels = []
    half = c // 2
    while half >= diag_block:
        blk = 2 * half
        is_right = (row_w & (blk - 1)) >= half
        same_blk = (row ^ col) < blk
        valid = same_blk & ((row & (blk - 1)) >= half) & ((col & (blk - 1)) < half)
        levels.append((half, is_right, valid))
        half //= 2
    diag = ((row ^ col) < diag_block) & (col <= row)
    return levels, diag


def _block_rows(b, blk, offset):
    return jnp.concatenate(
        [jnp.broadcast_to(b[s0 + offset:s0 + offset + 1], (blk, HEAD)) for s0 in range(0, b.shape[0], blk)],
        axis=0)


def _diag_scores_exact(q, k, b):
    c = q.shape[0]
    row8 = lax.broadcasted_iota(jnp.int32, (SUB, HEAD), 0)
    lane = lax.broadcasted_iota(jnp.int32, (SUB, c), 1)
    diag = []
    for i in range(c // SUB):
        bi, qi, ki = (t[SUB * i:SUB * (i + 1)] for t in (b, q, k))
        blk = jnp.zeros((SUB, c), F32)
        for s in range(SUB):
            arg = jnp.where(row8 >= s, bi - bi[s:s + 1], NEG_BIG)
            a = jnp.sum(qi * ki[s:s + 1] * jnp.exp(arg), axis=-1, keepdims=True)
            blk = jnp.where(lane == SUB * i + s, a, blk)
        diag.append(blk)
    return jnp.concatenate(diag, axis=0)


def _diag_scores_fast(q, k, b, diag_mask):
    d = b - _block_rows(b, FAST_BLOCK, FAST_BLOCK // 2 - 1)
    xq = (q * jnp.exp(d)).astype(BF16)
    xk = (k * jnp.exp(-d)).astype(BF16)
    return jnp.where(diag_mask, lax.dot_general(xq, xk, _NT, preferred_element_type=F32), 0.0)


def _chunk_scores(q, k, b, masks, exact):
    levels, diag_mask = masks
    sc = _diag_scores_exact(q, k, b) if exact else _diag_scores_fast(q, k, b, diag_mask)
    for half, is_right, valid in levels:
        mid = _block_rows(b, 2 * half, half - 1)
        x = (jnp.where(is_right, q, k) * jnp.exp(-jnp.abs(b - mid))).astype(BF16)
        sc = sc + jnp.where(valid, lax.dot_general(x, x, _NT, preferred_element_type=F32), 0.0)
    return sc


def _chunk_state(k, v, b, st):
    b_last = b[b.shape[0] - 1:, :]
    kd = (k * jnp.exp(b_last - b)).astype(BF16)
    return st * jnp.exp(b_last) + lax.dot_general(v.astype(BF16), kd, _TN, preferred_element_type=F32)


def _hgrn_out(o, g, zog):
    return _rms_rows(o, g) * _silu(zog)


def _hgrn_prompt_kernel(zq_ref, zf_ref, zi_ref, zog_ref, mf_ref, mi_ref, lbl_ref, g_ref,
                        o_ref, s_ref):
    n_chunks = zq_ref.shape[0] // CHUNK
    heads = [slice(h * HEAD, (h + 1) * HEAD) for h in range(HEADS_PER_STEP)]
    lbs = [_lower_bound(lbl_ref[:, hs]) for hs in heads]

    def meta_state(hs, lb):
        mk, mlf = _gates(mf_ref[:, hs], lb)
        return _chunk_state(mk, mi_ref[:, hs], _cumsum_rows(mlf), jnp.zeros((HEAD, HEAD), F32))

    def scan(exact, sts):
        masks = _chunk_masks(CHUNK, SUB if exact else FAST_BLOCK)

        def body(c, sts):
            rows = pl.ds(pl.multiple_of(c * CHUNK, CHUNK), CHUNK)
            new = []
            for hs, lb, st in zip(heads, lbs, sts):
                q = _silu(zq_ref[rows, hs])
                k, lf = _gates(zf_ref[rows, hs], lb)
                v = zi_ref[rows, hs]
                b = _cumsum_rows(lf)
                o = lax.dot_general((q * jnp.exp(b)).astype(BF16), st.astype(BF16), _NT,
                                    preferred_element_type=F32)
                sc = _chunk_scores(q, k, b, masks, exact)
                o = o + jnp.dot(sc.astype(BF16), v.astype(BF16), preferred_element_type=F32)
                o_ref[rows, hs] = _hgrn_out(o, g_ref[:, hs], zog_ref[rows, hs]).astype(o_ref.dtype)
                new.append(_chunk_state(k, v, b, st))
            return tuple(new)

        return lax.fori_loop(0, n_chunks, body, sts, unroll=CHUNK_UNROLL)

    min_log_lb = jnp.min(jnp.log(jnp.concatenate(lbs, axis=1)))
    sts = lax.cond(min_log_lb * (FAST_BLOCK // 2) >= -SAFE_EXP,
                   lambda sts: scan(False, sts), lambda sts: scan(True, sts),
                   tuple(meta_state(hs, lb) for hs, lb in zip(heads, lbs)))
    for h, st in enumerate(sts):
        s_ref[0, h] = st.T


def _hgrn_prompt(z, z_small, lb_logits, hgrn_g, batch, seq):
    width = HEADS_PER_STEP * HEAD
    hb = A_WIDTH // width
    col = lambda base: (lambda b, h: (b, base * hb + h))
    meta_blk = z_small.shape[0] // N_META - 1
    meta = lambda base: (lambda b, h: (meta_blk, base * hb + h))
    blocks = 4 * seq * width * 4 + seq * width * 2 + 2 * N_META * width * 4 + width * HEAD * 4
    return pl.pallas_call(
        _hgrn_prompt_kernel,
        grid=(batch, hb),
        in_specs=[pl.BlockSpec((seq, width), col(0)),
                  pl.BlockSpec((seq, width), col(1)),
                  pl.BlockSpec((seq, width), col(2)),
                  pl.BlockSpec((seq, width), col(3)),
                  pl.BlockSpec((N_META, width), meta(1)),
                  pl.BlockSpec((N_META, width), meta(2)),
                  pl.BlockSpec((lb_logits.shape[0], width), lambda b, h: (0, h)),
                  pl.BlockSpec((1, width), lambda b, h: (0, h))],
        out_specs=[pl.BlockSpec((seq, width), lambda b, h: (b, h)),
                   pl.BlockSpec((1, HEADS_PER_STEP, HEAD, HEAD), lambda b, h: (b, h, 0, 0))],
        out_shape=[jax.ShapeDtypeStruct((batch * seq, A_WIDTH), BF16),
                   jax.ShapeDtypeStruct((batch, A_HEADS, HEAD, HEAD), F32)],
        compiler_params=_params(_vmem_limit(blocks, 4 * 1024 * 1024), 2),
        name="hgrn_prompt",
    )(z, z, z, z, z_small, z_small, lb_logits, hgrn_g)


SAMPLE_ROWS = 8
SAMPLE_UNROLL = 4
SPLAT_ROWS = 16


def _hgrn_sample_kernel(z_ref, s_ref, lbl_ref, g_ref, o_ref, so_ref):
    row = lax.broadcasted_iota(jnp.int32, (SPLAT_ROWS, HEAD), 0)
    ones_f = jnp.where(row < 3, 1.0, 0.0)
    ones_q = jnp.where(row == 4, 1.0, 0.0)

    def per_seq(r, carry):
        zrow = z_ref[r]
        for h in range(A_HEADS):
            sl = lambda base: zrow[:, base * A_WIDTH + h * HEAD: base * A_WIDTH + (h + 1) * HEAD]
            lb = _lower_bound(lbl_ref[:, h * HEAD:(h + 1) * HEAD])
            q = _silu(sl(0))
            fg = lb + (1.0 - lb) * _sigmoid(sl(1))
            k = 1.0 - fg
            v = sl(2)
            f_hi = fg.astype(BF16).astype(F32)
            f_mid = (fg - f_hi).astype(BF16).astype(F32)
            f_lo = fg - f_hi - f_mid
            a = jnp.where(row == 0, f_hi, jnp.where(row == 1, f_mid, jnp.where(
                row == 2, f_lo, jnp.where(row == 3, k, jnp.where(row == 4, q, 0.0)))))
            bmat = jnp.concatenate([ones_f, ones_q, jnp.where(row == 3, v, 0.0)], axis=1)
            gm = lax.dot_general(a.astype(BF16), bmat.astype(BF16), _TN,
                                 preferred_element_type=F32)
            s_new = gm[:, 0:HEAD] * s_ref[r, h] + gm[:, 2 * HEAD:3 * HEAD]
            so_ref[r, h] = s_new
            o = jnp.sum(gm[:, HEAD:2 * HEAD] * s_new, axis=0, keepdims=True)
            og = sl(3)
            o_ref[r, :, h * HEAD:(h + 1) * HEAD] = _hgrn_out(
                o, g_ref[:, h * HEAD:(h + 1) * HEAD], og)
        return carry

    lax.fori_loop(0, SAMPLE_ROWS, per_seq, 0, unroll=SAMPLE_UNROLL)


def _hgrn_sample(z_small, state, lb_logits, hgrn_g):
    nb = state.shape[0]
    blocks = SAMPLE_ROWS * (IN_COLS * 4 + A_WIDTH * 4 + 2 * A_HEADS * HEAD * HEAD * 4)
    return pl.pallas_call(
        _hgrn_sample_kernel,
        grid=(nb // SAMPLE_ROWS,),
        in_specs=[pl.BlockSpec((SAMPLE_ROWS, 1, IN_COLS), lambda i: (i, 0, 0)),
                  pl.BlockSpec((SAMPLE_ROWS, A_HEADS, HEAD, HEAD), lambda i: (i, 0, 0, 0)),
                  pl.BlockSpec(lb_logits.shape, lambda i: (0, 0)),
                  pl.BlockSpec((1, A_WIDTH), lambda i: (0, 0))],
        out_specs=[pl.BlockSpec((SAMPLE_ROWS, 1, A_WIDTH), lambda i: (i, 0, 0)),
                   pl.BlockSpec((SAMPLE_ROWS, A_HEADS, HEAD, HEAD), lambda i: (i, 0, 0, 0))],
        out_shape=[jax.ShapeDtypeStruct((nb, 1, A_WIDTH), F32),
                   jax.ShapeDtypeStruct(state.shape, F32)],
        compiler_params=_params(_vmem_limit(blocks, 2 * 1024 * 1024), 1),
        name="hgrn_sample",
    )(z_small, state, lb_logits, hgrn_g)


def _group_norm_swish(c, gn_g, gn_b):
    mu = jnp.mean(c, axis=-1, keepdims=True)
    d = c - mu
    var = jnp.mean(d * d, axis=-1, keepdims=True)
    return _silu(d * lax.rsqrt(var + EPS) * gn_g + gn_b)


CONV_PAD = 32
CONV_TILE = 256
BF16_ROWS = 2 * V7X_SUBLANES


def _conv_prompt_kernel(ga_ref, gb_ref, ma_ref, mb_ref, w_ref, cb_ref, gg_ref, gbias_ref,
                        wg_ref, wu_ref, wd_ref, o_ref, buf_ref, wg_out, wu_out, wd_out, u_ref):
    wg_out[...] = wg_ref[...].astype(BF16)
    wu_out[...] = wu_ref[...].astype(BF16)
    wd_out[...] = wd_ref[...].astype(BF16)

    seq = ga_ref.shape[0]
    u_ref[0:CONV_PAD - N_META, :] = jnp.zeros((CONV_PAD - N_META, GROUP), F32)
    u_ref[CONV_PAD - N_META:CONV_PAD, :] = ma_ref[...] * _sigmoid(mb_ref[...])
    u_ref[CONV_PAD:, :] = ga_ref[...] * _sigmoid(gb_ref[...])
    bias = cb_ref[...]
    for t0 in range(0, seq, CONV_TILE):
        acc = jnp.zeros((CONV_TILE, GROUP), F32) + bias
        for j in range(CONV_W):
            start = t0 + CONV_PAD - HIST + j
            acc = acc + w_ref[j:j + 1, :] * u_ref[start:start + CONV_TILE, :]
        o_ref[t0:t0 + CONV_TILE, :] = _group_norm_swish(acc, gg_ref[...], gbias_ref[...]).astype(
            o_ref.dtype)
    buf_ref[0] = u_ref[CONV_PAD + seq - HIST:CONV_PAD + seq, :]


def _conv_prompt(z, z_small, conv_w, conv_b, gn_g, gn_b, w_gate, w_up, w_down, batch, seq):
    ga0 = 4 * A_WIDTH // GROUP
    gb0 = ga0 + B_WIDTH // GROUP
    meta_blk = z_small.shape[0] // N_META - 1
    vec = pl.BlockSpec((1, GROUP), lambda b, g: (0, g))
    steps = batch * B_GROUPS
    assert D_MODEL % (BF16_ROWS * steps) == 0 and D_FF % (BF16_ROWS * steps) == 0
    up_rows, down_rows = D_MODEL // steps, D_FF // steps
    slab = lambda rows, cols: pl.BlockSpec((rows, cols), lambda b, g: (b * B_GROUPS + g, 0))
    blocks = (2 * seq * GROUP * 4 + seq * GROUP * 2 + 32 * GROUP * 4 * 4
              + (2 * up_rows * D_FF + down_rows * D_MODEL) * (4 + 2))
    return pl.pallas_call(
        _conv_prompt_kernel,
        grid=(batch, B_GROUPS),
        in_specs=[pl.BlockSpec((seq, GROUP), lambda b, g: (b, ga0 + g)),
                  pl.BlockSpec((seq, GROUP), lambda b, g: (b, gb0 + g)),
                  pl.BlockSpec((N_META, GROUP), lambda b, g: (meta_blk, ga0 + g)),
                  pl.BlockSpec((N_META, GROUP), lambda b, g: (meta_blk, gb0 + g)),
                  pl.BlockSpec((CONV_W, GROUP), lambda b, g: (0, g)),
                  vec, vec, vec,
                  slab(up_rows, D_FF), slab(up_rows, D_FF), slab(down_rows, D_MODEL)],
        out_specs=[pl.BlockSpec((seq, GROUP), lambda b, g: (b, g)),
                   pl.BlockSpec((1, HIST, GROUP), lambda b, g: (b, 0, g)),
                   slab(up_rows, D_FF), slab(up_rows, D_FF), slab(down_rows, D_MODEL)],
        out_shape=[jax.ShapeDtypeStruct((batch * seq, B_WIDTH), BF16),
                   jax.ShapeDtypeStruct((batch, HIST, B_WIDTH), F32),
                   jax.ShapeDtypeStruct((D_MODEL, D_FF), BF16),
                   jax.ShapeDtypeStruct((D_MODEL, D_FF), BF16),
                   jax.ShapeDtypeStruct((D_FF, D_MODEL), BF16)],
        scratch_shapes=[pltpu.VMEM((CONV_PAD + seq, GROUP), F32)],
        compiler_params=_params(_vmem_limit(blocks, (CONV_PAD + seq) * GROUP * 4 + 4 * 1024 * 1024), 2),
        name="conv_prompt",
    )(z, z, z_small, z_small, conv_w, conv_b, gn_g, gn_b, w_gate, w_up, w_down)


CONV_SEQS = 16


def _conv_sample_kernel(ga_ref, gb_ref, st_ref, w_ref, cb_ref, gg_ref, gbias_ref, o_ref, buf_ref):
    u = ga_ref[...] * _sigmoid(gb_ref[...])
    c = w_ref[HIST:CONV_W, :] * u + cb_ref[...]
    for j in range(HIST):
        c = c + w_ref[j:j + 1, :] * st_ref[j]
    for g in range(B_GROUPS):
        sl = slice(g * GROUP, (g + 1) * GROUP)
        o_ref[:, sl] = _group_norm_swish(c[:, sl], gg_ref[:, sl], gbias_ref[:, sl])
    for j in range(HIST - 1):
        buf_ref[j] = st_ref[j + 1]
    buf_ref[HIST - 1] = u


def _conv_sample(z_small, state_t, conv_w, conv_b, gn_g, gn_b):
    nb = state_t.shape[1]
    ga0 = 4 * A_WIDTH // B_WIDTH
    vec = pl.BlockSpec((1, B_WIDTH), lambda i: (0, 0))
    blocks = CONV_SEQS * B_WIDTH * 4 * (3 + 2 * HIST) + 32 * B_WIDTH * 4
    return pl.pallas_call(
        _conv_sample_kernel,
        grid=(nb // CONV_SEQS,),
        in_specs=[pl.BlockSpec((CONV_SEQS, B_WIDTH), lambda i: (i, ga0)),
                  pl.BlockSpec((CONV_SEQS, B_WIDTH), lambda i: (i, ga0 + 1)),
                  pl.BlockSpec((HIST, CONV_SEQS, B_WIDTH), lambda i: (0, i, 0)),
                  pl.BlockSpec((CONV_W, B_WIDTH), lambda i: (0, 0)),
                  vec, vec, vec],
        out_specs=[pl.BlockSpec((CONV_SEQS, B_WIDTH), lambda i: (i, 0)),
                   pl.BlockSpec((HIST, CONV_SEQS, B_WIDTH), lambda i: (0, i, 0))],
        out_shape=[jax.ShapeDtypeStruct((nb, B_WIDTH), F32),
                   jax.ShapeDtypeStruct(state_t.shape, F32)],
        compiler_params=_params(_vmem_limit(blocks, 2 * 1024 * 1024), 1),
        name="conv_sample",
    )(z_small, z_small, state_t, conv_w, conv_b, gn_g, gn_b)


def _outproj_kernel(oa_ref, ob_ref, x_ref, wa_ref, wb_ref, g_ref, h_ref, hf_ref):
    h = x_ref[...]
    h = h + jnp.dot(oa_ref[...].astype(BF16), wa_ref[...], preferred_element_type=F32)
    h = h + jnp.dot(ob_ref[...].astype(BF16), wb_ref[...], preferred_element_type=F32)
    h_ref[...] = h
    hf_ref[...] = _rms_rows(h, g_ref[...]).astype(BF16)


def _outproj(oa, ob, x, w_bf16, g, tm):
    m = x.shape[0]
    act = oa.dtype.itemsize
    blocks = 2 * tm * A_WIDTH * act + tm * D_MODEL * (4 + 4 + 2) + D_MODEL * D_MODEL * 2
    return pl.pallas_call(
        _outproj_kernel,
        grid=(m // tm,),
        in_specs=[pl.BlockSpec((tm, A_WIDTH), lambda i: (i, 0)),
                  pl.BlockSpec((tm, B_WIDTH), lambda i: (i, 0)),
                  pl.BlockSpec((tm, D_MODEL), lambda i: (i, 0)),
                  pl.BlockSpec((A_WIDTH, D_MODEL), lambda i: (0, 0)),
                  pl.BlockSpec((B_WIDTH, D_MODEL), lambda i: (1, 0)),
                  pl.BlockSpec((1, D_MODEL), lambda i: (0, 0))],
        out_specs=[pl.BlockSpec((tm, D_MODEL), lambda i: (i, 0)),
                   pl.BlockSpec((tm, D_MODEL), lambda i: (i, 0))],
        out_shape=[jax.ShapeDtypeStruct((m, D_MODEL), F32),
                   jax.ShapeDtypeStruct((m, D_MODEL), BF16)],
        compiler_params=_params(_vmem_limit(blocks, 2 * tm * D_MODEL * 4), 1),
        name="outproj",
    )(oa, ob, x, w_bf16, w_bf16, g)


def _ffn_kernel(hf_ref, h_ref, wg_ref, wu_ref, wd_ref, g_ref, y_ref, acc_ref):
    f = pl.program_id(1)

    @pl.when(f == 0)
    def _():
        acc_ref[...] = h_ref[...]

    hf = hf_ref[...]
    gate = jnp.dot(hf, wg_ref[...], preferred_element_type=F32)
    up = jnp.dot(hf, wu_ref[...], preferred_element_type=F32)
    act = (_silu(gate) * up).astype(BF16)
    acc_ref[...] += jnp.dot(act, wd_ref[...], preferred_element_type=F32)

    @pl.when(f == pl.num_programs(1) - 1)
    def _():
        y_ref[...] = _rms_rows(acc_ref[...], g_ref[...])


def _ffn(hf, h1, wg, wu, wd, g, tm, tf):
    m = hf.shape[0]
    blocks = tm * D_MODEL * (2 + 4 + 4) + 3 * D_MODEL * tf * 2
    resident = tm * D_MODEL * 4 + 3 * tm * tf * 4
    return pl.pallas_call(
        _ffn_kernel,
        grid=(m // tm, D_FF // tf),
        in_specs=[pl.BlockSpec((tm, D_MODEL), lambda i, f: (i, 0)),
                  pl.BlockSpec((tm, D_MODEL), lambda i, f: (i, 0)),
                  pl.BlockSpec((D_MODEL, tf), lambda i, f: (0, f)),
                  pl.BlockSpec((D_MODEL, tf), lambda i, f: (0, f)),
                  pl.BlockSpec((tf, D_MODEL), lambda i, f: (f, 0)),
                  pl.BlockSpec((1, D_MODEL), lambda i, f: (0, 0))],
        out_specs=pl.BlockSpec((tm, D_MODEL), lambda i, f: (i, 0)),
        out_shape=jax.ShapeDtypeStruct((m, D_MODEL), F32),
        scratch_shapes=[pltpu.VMEM((tm, D_MODEL), F32)],
        compiler_params=_params(_vmem_limit(blocks, resident), 2),
        name="ffn",
    )(hf, h1, wg, wu, wd, g)


def kernel(x_prompt, x_sample, state_hgrn, state_conv, meta_tokens, norm_mix_g, w_in, lb_logits,
           hgrn_norm_g, conv_w, conv_b, gn_g, gn_b, w_out, norm_ffn_g, w_ffn_gate, w_ffn_up,
           w_ffn_down, norm_final_g):
    batch, seq, _ = x_prompt.shape
    nb = x_sample.shape[0]
    assert x_sample.shape[1] == 1 and norm_mix_g.shape[0] == 1
    assert seq % (CHUNK * CHUNK_UNROLL) == 0 and seq % CONV_TILE == 0
    assert nb % N_META == 0 and nb % CONV_SEQS == 0 and nb % SAMPLE_ROWS == 0

    w_in_b = w_in[0].astype(BF16)
    w_out_b = w_out[0].astype(BF16)
    g_final = norm_final_g[None, :]

    xp = x_prompt.reshape(batch * seq, D_MODEL)
    xs = x_sample.reshape(nb, D_MODEL)
    x_small = jnp.concatenate([xs, meta_tokens], axis=0)

    z = _inproj(xp, norm_mix_g, w_in_b, tm=1024, tn=1024)
    z_small = _inproj(x_small, norm_mix_g, w_in_b, tm=x_small.shape[0], tn=1024)

    oa_p, s_p = _hgrn_prompt(z, z_small, lb_logits, hgrn_norm_g, batch, seq)
    ob_p, c_p, wg_b, wu_b, wd_b = _conv_prompt(z, z_small, conv_w[0], conv_b, gn_g, gn_b,
                                               w_ffn_gate[0], w_ffn_up[0], w_ffn_down[0], batch, seq)
    z_s = z_small[:nb].reshape(nb, 1, IN_COLS)
    oa_s, s_s = _hgrn_sample(z_s, state_hgrn[0], lb_logits, hgrn_norm_g)
    ob_s, c_s = _conv_sample(z_small, jnp.transpose(state_conv[0], (1, 0, 2)), conv_w[0], conv_b, gn_g, gn_b)
    c_s = jnp.transpose(c_s, (1, 0, 2))
    oa_s = oa_s.reshape(nb, A_WIDTH)

    h_p, hf_p = _outproj(oa_p, ob_p, xp, w_out_b, norm_ffn_g, tm=256)
    h_s, hf_s = _outproj(oa_s, ob_s, xs, w_out_b, norm_ffn_g, tm=nb)

    y_p = _ffn(hf_p, h_p, wg_b, wu_b, wd_b, g_final, tm=512, tf=512)
    y_s = _ffn(hf_s, h_s, wg_b, wu_b, wd_b, g_final, tm=nb, tf=512)

    return (y_p.reshape(batch, seq, D_MODEL), y_s.reshape(nb, 1, D_MODEL),
            s_p[None], c_p[None], s_s[None], c_s[None])
```

```python
import jax
import jax.numpy as jnp
from jax import lax
from jax.experimental import pallas as pl
from jax.experimental.pallas import tpu as pltpu

F32 = jnp.float32
BF16 = jnp.bfloat16

D_MODEL = 2048
N_META = 16
A_WIDTH = 1024
B_WIDTH = 1024
HEAD = 128
A_HEADS = A_WIDTH // HEAD
CONV_W = 31
HIST = CONV_W - 1
B_GROUPS = 8
GROUP = B_WIDTH // B_GROUPS
D_FF = 5632
IN_COLS = 4 * A_WIDTH + 2 * B_WIDTH
EPS = 1e-6

V7X_LANES = 128
V7X_SUBLANES = 8
V7X_VMEM_BYTES = 64 * 1024 * 1024
V7X_VMEM_USABLE = V7X_VMEM_BYTES - 8 * 1024 * 1024

CHUNK = 64
SUB = V7X_SUBLANES
NEG_BIG = -1e30
FAST_BLOCK = 16
SAFE_EXP = 60.0
HEADS_PER_STEP = 4
CHUNK_UNROLL = 2

_NT = (((1,), (1,)), ((), ()))
_TN = (((0,), (0,)), ((), ()))


def _vmem_limit(pipelined_block_bytes, resident_bytes):
    est = 2 * pipelined_block_bytes + resident_bytes
    return int(min(max(est, 16 * 1024 * 1024), V7X_VMEM_USABLE))


def _params(vmem_bytes, ndims):
    return pltpu.CompilerParams(dimension_semantics=("arbitrary",) * ndims,
                                vmem_limit_bytes=vmem_bytes)


def _sigmoid(x):
    return 1.0 / (1.0 + jnp.exp(-x))


def _silu(x):
    return x * _sigmoid(x)


def _rms_rows(x, g):
    return x * lax.rsqrt(jnp.mean(x * x, axis=-1, keepdims=True) + EPS) * g


def _inproj_kernel(x_ref, g_ref, w_ref, z_ref, xn_ref):
    @pl.when(pl.program_id(1) == 0)
    def _():
        xn_ref[...] = _rms_rows(x_ref[...], g_ref[...]).astype(BF16)

    z_ref[...] = jnp.dot(xn_ref[...], w_ref[...], preferred_element_type=F32)


def _inproj(x, g, w_bf16, tm, tn):
    m = x.shape[0]
    n = w_bf16.shape[1]
    blocks = tm * D_MODEL * 4 + D_MODEL * tn * 2 + tm * tn * 4
    resident = tm * D_MODEL * 2 + tm * tn * 4
    return pl.pallas_call(
        _inproj_kernel,
        grid=(m // tm, n // tn),
        in_specs=[pl.BlockSpec((tm, D_MODEL), lambda i, j: (i, 0)),
                  pl.BlockSpec((1, D_MODEL), lambda i, j: (0, 0)),
                  pl.BlockSpec((D_MODEL, tn), lambda i, j: (0, j))],
        out_specs=pl.BlockSpec((tm, tn), lambda i, j: (i, j)),
        out_shape=jax.ShapeDtypeStruct((m, n), F32),
        scratch_shapes=[pltpu.VMEM((tm, D_MODEL), BF16)],
        compiler_params=_params(_vmem_limit(blocks, resident), 2),
        name="inproj",
    )(x, g, w_bf16)


def _lower_bound(lb_logits):
    m = jnp.max(lb_logits, axis=0, keepdims=True)
    e = jnp.exp(lb_logits - m)
    return e[0:1, :] / jnp.sum(e, axis=0, keepdims=True)


def _cumsum_rows(x):
    n = x.shape[0]
    row = lax.broadcasted_iota(jnp.int32, x.shape, 0)
    s = 1
    while s < n:
        x = x + jnp.where(row >= s, pltpu.roll(x, s, 0), 0.0)
        s *= 2
    return x


def _gates(zf, lb):
    fg = lb + (1.0 - lb) * _sigmoid(zf)
    return 1.0 - fg, jnp.log(fg)


def _chunk_masks(c, diag_block):
    row_w = lax.broadcasted_iota(jnp.int32, (c, HEAD), 0)
    row = lax.broadcasted_iota(jnp.int32, (c, c), 0)
    col = lax.broadcasted_iota(jnp.int32, (c, c), 1)
    levels = []
    half = c // 2
    while half >= diag_block:
        blk = 2 * half
        is_right = (row_w & (blk - 1)) >= half
        same_blk = (row ^ col) < blk
        valid = same_blk & ((row & (blk - 1)) >= half) & ((col & (blk - 1)) < half)
        levels.append((half, is_right, valid))
        half //= 2
    diag = ((row ^ col) < diag_block) & (col <= row)
    return levels, diag


def _block_rows(b, blk, offset):
    return jnp.concatenate(
        [jnp.broadcast_to(b[s0 + offset:s0 + offset + 1], (blk, HEAD)) for s0 in range(0, b.shape[0], blk)],
        axis=0)


def _diag_scores_exact(q, k, b):
    c = q.shape[0]
    row8 = lax.broadcasted_iota(jnp.int32, (SUB, HEAD), 0)
    lane = lax.broadcasted_iota(jnp.int32, (SUB, c), 1)
    diag = []
    for i in range(c // SUB):
        bi, qi, ki = (t[SUB * i:SUB * (i + 1)] for t in (b, q, k))
        blk = jnp.zeros((SUB, c), F32)
        for s in range(SUB):
            arg = jnp.where(row8 >= s, bi - bi[s:s + 1], NEG_BIG)
            a = jnp.sum(qi * ki[s:s + 1] * jnp.exp(arg), axis=-1, keepdims=True)
            blk = jnp.where(lane == SUB * i + s, a, blk)
        diag.append(blk)
    return jnp.concatenate(diag, axis=0)


def _diag_scores_fast(q, k, b, diag_mask):
    d = b - _block_rows(b, FAST_BLOCK, FAST_BLOCK // 2 - 1)
    xq = (q * jnp.exp(d)).astype(BF16)
    xk = (k * jnp.exp(-d)).astype(BF16)
    return jnp.where(diag_mask, lax.dot_general(xq, xk, _NT, preferred_element_type=F32), 0.0)


def _chunk_scores(q, k, b, masks, exact):
    levels, diag_mask = masks
    sc = _diag_scores_exact(q, k, b) if exact else _diag_scores_fast(q, k, b, diag_mask)
    for half, is_right, valid in levels:
        mid = _block_rows(b, 2 * half, half - 1)
        x = (jnp.where(is_right, q, k) * jnp.exp(-jnp.abs(b - mid))).astype(BF16)
        sc = sc + jnp.where(valid, lax.dot_general(x, x, _NT, preferred_element_type=F32), 0.0)
    return sc


def _chunk_state(k, v, b, st):
    b_last = b[b.shape[0] - 1:, :]
    kd = (k * jnp.exp(b_last - b)).astype(BF16)
    return st * jnp.exp(b_last) + lax.dot_general(v.astype(BF16), kd, _TN, preferred_element_type=F32)


def _hgrn_out(o, g, zog):
    return _rms_rows(o, g) * _silu(zog)


def _hgrn_prompt_kernel(zq_ref, zf_ref, zi_ref, zog_ref, mf_ref, mi_ref, lbl_ref, g_ref,
                        o_ref, s_ref):
    n_chunks = zq_ref.shape[0] // CHUNK
    heads = [slice(h * HEAD, (h + 1) * HEAD) for h in range(HEADS_PER_STEP)]
    lbs = [_lower_bound(lbl_ref[:, hs]) for hs in heads]

    def meta_state(hs, lb):
        mk, mlf = _gates(mf_ref[:, hs], lb)
        return _chunk_state(mk, mi_ref[:, hs], _cumsum_rows(mlf), jnp.zeros((HEAD, HEAD), F32))

    def scan(exact, sts):
        masks = _chunk_masks(CHUNK, SUB if exact else FAST_BLOCK)

        def body(c, sts):
            rows = pl.ds(pl.multiple_of(c * CHUNK, CHUNK), CHUNK)
            new = []
            for hs, lb, st in zip(heads, lbs, sts):
                q = _silu(zq_ref[rows, hs])
                k, lf = _gates(zf_ref[rows, hs], lb)
                v = zi_ref[rows, hs]
                b = _cumsum_rows(lf)
                o = lax.dot_general((q * jnp.exp(b)).astype(BF16), st.astype(BF16), _NT,
                                    preferred_element_type=F32)
                sc = _chunk_scores(q, k, b, masks, exact)
                o = o + jnp.dot(sc.astype(BF16), v.astype(BF16), preferred_element_type=F32)
                o_ref[rows, hs] = _hgrn_out(o, g_ref[:, hs], zog_ref[rows, hs]).astype(o_ref.dtype)
                new.append(_chunk_state(k, v, b, st))
            return tuple(new)

        return lax.fori_loop(0, n_chunks, body, sts, unroll=CHUNK_UNROLL)

    min_log_lb = jnp.min(jnp.log(jnp.concatenate(lbs, axis=1)))
    sts = lax.cond(min_log_lb * (FAST_BLOCK // 2) >= -SAFE_EXP,
                   lambda sts: scan(False, sts), lambda sts: scan(True, sts),
                   tuple(meta_state(hs, lb) for hs, lb in zip(heads, lbs)))
    for h, st in enumerate(sts):
        s_ref[0, h] = st.T


def _hgrn_prompt(z, z_small, lb_logits, hgrn_g, batch, seq):
    width = HEADS_PER_STEP * HEAD
    hb = A_WIDTH // width
    col = lambda base: (lambda b, h: (b, base * hb + h))
    meta_blk = z_small.shape[0] // N_META - 1
    meta = lambda base: (lambda b, h: (meta_blk, base * hb + h))
    blocks = 4 * seq * width * 4 + seq * width * 2 + 2 * N_META * width * 4 + width * HEAD * 4
    return pl.pallas_call(
        _hgrn_prompt_kernel,
        grid=(batch, hb),
        in_specs=[pl.BlockSpec((seq, width), col(0)),
                  pl.BlockSpec((seq, width), col(1)),
                  pl.BlockSpec((seq, width), col(2)),
                  pl.BlockSpec((seq, width), col(3)),
                  pl.BlockSpec((N_META, width), meta(1)),
                  pl.BlockSpec((N_META, width), meta(2)),
                  pl.BlockSpec((lb_logits.shape[0], width), lambda b, h: (0, h)),
                  pl.BlockSpec((1, width), lambda b, h: (0, h))],
        out_specs=[pl.BlockSpec((seq, width), lambda b, h: (b, h)),
                   pl.BlockSpec((1, HEADS_PER_STEP, HEAD, HEAD), lambda b, h: (b, h, 0, 0))],
        out_shape=[jax.ShapeDtypeStruct((batch * seq, A_WIDTH), BF16),
                   jax.ShapeDtypeStruct((batch, A_HEADS, HEAD, HEAD), F32)],
        compiler_params=_params(_vmem_limit(blocks, 4 * 1024 * 1024), 2),
        name="hgrn_prompt",
    )(z, z, z, z, z_small, z_small, lb_logits, hgrn_g)


SAMPLE_ROWS = 8
SAMPLE_UNROLL = 4
SPLAT_ROWS = 16


def _hgrn_sample_kernel(z_ref, s_ref, lbl_ref, g_ref, o_ref, so_ref):
    row = lax.broadcasted_iota(jnp.int32, (SPLAT_ROWS, HEAD), 0)
    ones_f = jnp.where(row < 3, 1.0, 0.0)
    ones_q = jnp.where(row == 4, 1.0, 0.0)

    def per_seq(r, carry):
        zrow = z_ref[r]
        for h in range(A_HEADS):
            sl = lambda base: zrow[:, base * A_WIDTH + h * HEAD: base * A_WIDTH + (h + 1) * HEAD]
            lb = _lower_bound(lbl_ref[:, h * HEAD:(h + 1) * HEAD])
            q = _silu(sl(0))
            fg = lb + (1.0 - lb) * _sigmoid(sl(1))
            k = 1.0 - fg
            v = sl(2)
            f_hi = fg.astype(BF16).astype(F32)
            f_mid = (fg - f_hi).astype(BF16).astype(F32)
            f_lo = fg - f_hi - f_mid
            a = jnp.where(row == 0, f_hi, jnp.where(row == 1, f_mid, jnp.where(
                row == 2, f_lo, jnp.where(row == 3, k, jnp.where(row == 4, q, 0.0)))))
            bmat = jnp.concatenate([ones_f, ones_q, jnp.where(row == 3, v, 0.0)], axis=1)
            gm = lax.dot_general(a.astype(BF16), bmat.astype(BF16), _TN,
                                 preferred_element_type=F32)
            s_new = gm[:, 0:HEAD] * s_ref[r, h] + gm[:, 2 * HEAD:3 * HEAD]
            so_ref[r, h] = s_new
            o = jnp.sum(gm[:, HEAD:2 * HEAD] * s_new, axis=0, keepdims=True)
            og = sl(3)
            o_ref[r, :, h * HEAD:(h + 1) * HEAD] = _hgrn_out(
                o, g_ref[:, h * HEAD:(h + 1) * HEAD], og)
        return carry

    lax.fori_loop(0, SAMPLE_ROWS, per_seq, 0, unroll=SAMPLE_UNROLL)


def _hgrn_sample(z_small, state, lb_logits, hgrn_g):
    nb = state.shape[0]
    blocks = SAMPLE_ROWS * (IN_COLS * 4 + A_WIDTH * 4 + 2 * A_HEADS * HEAD * HEAD * 4)
    return pl.pallas_call(
        _hgrn_sample_kernel,
        grid=(nb // SAMPLE_ROWS,),
        in_specs=[pl.BlockSpec((SAMPLE_ROWS, 1, IN_COLS), lambda i: (i, 0, 0)),
                  pl.BlockSpec((SAMPLE_ROWS, A_HEADS, HEAD, HEAD), lambda i: (i, 0, 0, 0)),
                  pl.BlockSpec(lb_logits.shape, lambda i: (0, 0)),
                  pl.BlockSpec((1, A_WIDTH), lambda i: (0, 0))],
        out_specs=[pl.BlockSpec((SAMPLE_ROWS, 1, A_WIDTH), lambda i: (i, 0, 0)),
                   pl.BlockSpec((SAMPLE_ROWS, A_HEADS, HEAD, HEAD), lambda i: (i, 0, 0, 0))],
        out_shape=[jax.ShapeDtypeStruct((nb, 1, A_WIDTH), F32),
                   jax.ShapeDtypeStruct(state.shape, F32)],
        compiler_params=_params(_vmem_limit(blocks, 2 * 1024 * 1024), 1),
        name="hgrn_sample",
    )(z_small, state, lb_logits, hgrn_g)


def _group_norm_swish(c, gn_g, gn_b):
    mu = jnp.mean(c, axis=-1, keepdims=True)
    d = c - mu
    var = jnp.mean(d * d, axis=-1, keepdims=True)
    return _silu(d * lax.rsqrt(var + EPS) * gn_g + gn_b)


CONV_PAD = 32
CONV_TILE = 256
BF16_ROWS = 2 * V7X_SUBLANES


def _conv_prompt_kernel(ga_ref, gb_ref, ma_ref, mb_ref, w_ref, cb_ref, gg_ref, gbias_ref,
                        wg_ref, wu_ref, wd_ref, o_ref, buf_ref, wg_out, wu_out, wd_out, u_ref):
    wg_out[...] = wg_ref[...].astype(BF16)
    wu_out[...] = wu_ref[...].astype(BF16)
    wd_out[...] = wd_ref[...].astype(BF16)

    seq = ga_ref.shape[0]
    u_ref[0:CONV_PAD - N_META, :] = jnp.zeros((CONV_PAD - N_META, GROUP), F32)
    u_ref[CONV_PAD - N_META:CONV_PAD, :] = ma_ref[...] * _sigmoid(mb_ref[...])
    u_ref[CONV_PAD:, :] = ga_ref[...] * _sigmoid(gb_ref[...])
    bias = cb_ref[...]
    for t0 in range(0, seq, CONV_TILE):
        acc = jnp.zeros((CONV_TILE, GROUP), F32) + bias
        for j in range(CONV_W):
            start = t0 + CONV_PAD - HIST + j
            acc = acc + w_ref[j:j + 1, :] * u_ref[start:start + CONV_TILE, :]
        o_ref[t0:t0 + CONV_TILE, :] = _group_norm_swish(acc, gg_ref[...], gbias_ref[...]).astype(
            o_ref.dtype)
    buf_ref[0] = u_ref[CONV_PAD + seq - HIST:CONV_PAD + seq, :]


def _conv_prompt(z, z_small, conv_w, conv_b, gn_g, gn_b, w_gate, w_up, w_down, batch, seq):
    ga0 = 4 * A_WIDTH // GROUP
    gb0 = ga0 + B_WIDTH // GROUP
    meta_blk = z_small.shape[0] // N_META - 1
    vec = pl.BlockSpec((1, GROUP), lambda b, g: (0, g))
    steps = batch * B_GROUPS
    assert D_MODEL % (BF16_ROWS * steps) == 0 and D_FF % (BF16_ROWS * steps) == 0
    up_rows, down_rows = D_MODEL // steps, D_FF // steps
    slab = lambda rows, cols: pl.BlockSpec((rows, cols), lambda b, g: (b * B_GROUPS + g, 0))
    blocks = (2 * seq * GROUP * 4 + seq * GROUP * 2 + 32 * GROUP * 4 * 4
              + (2 * up_rows * D_FF + down_rows * D_MODEL) * (4 + 2))
    return pl.pallas_call(
        _conv_prompt_kernel,
        grid=(batch, B_GROUPS),
        in_specs=[pl.BlockSpec((seq, GROUP), lambda b, g: (b, ga0 + g)),
                  pl.BlockSpec((seq, GROUP), lambda b, g: (b, gb0 + g)),
                  pl.BlockSpec((N_META, GROUP), lambda b, g: (meta_blk, ga0 + g)),
                  pl.BlockSpec((N_META, GROUP), lambda b, g: (meta_blk, gb0 + g)),
                  pl.BlockSpec((CONV_W, GROUP), lambda b, g: (0, g)),
                  vec, vec, vec,
                  slab(up_rows, D_FF), slab(up_rows, D_FF), slab(down_rows, D_MODEL)],
        out_specs=[pl.BlockSpec((seq, GROUP), lambda b, g: (b, g)),
                   pl.BlockSpec((1, HIST, GROUP), lambda b, g: (b, 0, g)),
                   slab(up_rows, D_FF), slab(up_rows, D_FF), slab(down_rows, D_MODEL)],
        out_shape=[jax.ShapeDtypeStruct((batch * seq, B_WIDTH), BF16),
                   jax.ShapeDtypeStruct((batch, HIST, B_WIDTH), F32),
                   jax.ShapeDtypeStruct((D_MODEL, D_FF), BF16),
                   jax.ShapeDtypeStruct((D_MODEL, D_FF), BF16),
                   jax.ShapeDtypeStruct((D_FF, D_MODEL), BF16)],
        scratch_shapes=[pltpu.VMEM((CONV_PAD + seq, GROUP), F32)],
        compiler_params=_params(_vmem_limit(blocks, (CONV_PAD + seq) * GROUP * 4 + 4 * 1024 * 1024), 2),
        name="conv_prompt",
    )(z, z, z_small, z_small, conv_w, conv_b, gn_g, gn_b, w_gate, w_up, w_down)


CONV_SEQS = 16


def _conv_sample_kernel(ga_ref, gb_ref, st_ref, w_ref, cb_ref, gg_ref, gbias_ref, o_ref, buf_ref):
    u = ga_ref[...] * _sigmoid(gb_ref[...])
    c = w_ref[HIST:CONV_W, :] * u + cb_ref[...]
    for j in range(HIST):
        c = c + w_ref[j:j + 1, :] * st_ref[j]
    for g in range(B_GROUPS):
        sl = slice(g * GROUP, (g + 1) * GROUP)
        o_ref[:, sl] = _group_norm_swish(c[:, sl], gg_ref[:, sl], gbias_ref[:, sl])
    for j in range(HIST - 1):
        buf_ref[j] = st_ref[j + 1]
    buf_ref[HIST - 1] = u


def _conv_sample(z_small, state_t, conv_w, conv_b, gn_g, gn_b):
    nb = state_t.shape[1]
    ga0 = 4 * A_WIDTH // B_WIDTH
    vec = pl.BlockSpec((1, B_WIDTH), lambda i: (0, 0))
    blocks = CONV_SEQS * B_WIDTH * 4 * (3 + 2 * HIST) + 32 * B_WIDTH * 4
    return pl.pallas_call(
        _conv_sample_kernel,
        grid=(nb // CONV_SEQS,),
        in_specs=[pl.BlockSpec((CONV_SEQS, B_WIDTH), lambda i: (i, ga0)),
                  pl.BlockSpec((CONV_SEQS, B_WIDTH), lambda i: (i, ga0 + 1)),
                  pl.BlockSpec((HIST, CONV_SEQS, B_WIDTH), lambda i: (0, i, 0)),
                  pl.BlockSpec((CONV_W, B_WIDTH), lambda i: (0, 0)),
                  vec, vec, vec],
        out_specs=[pl.BlockSpec((CONV_SEQS, B_WIDTH), lambda i: (i, 0)),
                   pl.BlockSpec((HIST, CONV_SEQS, B_WIDTH), lambda i: (0, i, 0))],
        out_shape=[jax.ShapeDtypeStruct((nb, B_WIDTH), F32),
                   jax.ShapeDtypeStruct(state_t.shape, F32)],
        compiler_params=_params(_vmem_limit(blocks, 2 * 1024 * 1024), 1),
        name="conv_sample",
    )(z_small, z_small, state_t, conv_w, conv_b, gn_g, gn_b)


def _outproj_kernel(oa_ref, ob_ref, x_ref, wa_ref, wb_ref, g_ref, h_ref, hf_ref):
    h = x_ref[...]
    h = h + jnp.dot(oa_ref[...].astype(BF16), wa_ref[...], preferred_element_type=F32)
    h = h + jnp.dot(ob_ref[...].astype(BF16), wb_ref[...], preferred_element_type=F32)
    h_ref[...] = h
    hf_ref[...] = _rms_rows(h, g_ref[...]).astype(BF16)


def _outproj(oa, ob, x, w_bf16, g, tm):
    m = x.shape[0]
    act = oa.dtype.itemsize
    blocks = 2 * tm * A_WIDTH * act + tm * D_MODEL * (4 + 4 + 2) + D_MODEL * D_MODEL * 2
    return pl.pallas_call(
        _outproj_kernel,
        grid=(m // tm,),
        in_specs=[pl.BlockSpec((tm, A_WIDTH), lambda i: (i, 0)),
                  pl.BlockSpec((tm, B_WIDTH), lambda i: (i, 0)),
                  pl.BlockSpec((tm, D_MODEL), lambda i: (i, 0)),
                  pl.BlockSpec((A_WIDTH, D_MODEL), lambda i: (0, 0)),
                  pl.BlockSpec((B_WIDTH, D_MODEL), lambda i: (1, 0)),
                  pl.BlockSpec((1, D_MODEL), lambda i: (0, 0))],
        out_specs=[pl.BlockSpec((tm, D_MODEL), lambda i: (i, 0)),
                   pl.BlockSpec((tm, D_MODEL), lambda i: (i, 0))],
        out_shape=[jax.ShapeDtypeStruct((m, D_MODEL), F32),
                   jax.ShapeDtypeStruct((m, D_MODEL), BF16)],
        compiler_params=_params(_vmem_limit(blocks, 2 * tm * D_MODEL * 4), 1),
        name="outproj",
    )(oa, ob, x, w_bf16, w_bf16, g)


RESID_ROWS = 128


def _swiglu_down(hf, wg_ref, wu_ref, wd_ref):
    gate = jnp.dot(hf, wg_ref[...], preferred_element_type=F32)
    up = jnp.dot(hf, wu_ref[...], preferred_element_type=F32)
    return jnp.dot((_silu(gate) * up).astype(BF16), wd_ref[...], preferred_element_type=F32)


def _ffn_kernel(hf_ref, h_ref, hfs_ref, hs_ref, wg_ref, wu_ref, wd_ref, g_ref, y_ref, ys_ref):
    f = pl.program_id(1)
    last = pl.num_programs(1) - 1
    resid = h_ref.shape[0]

    @pl.when(f == 0)
    def _():
        y_ref[...] = jnp.zeros(y_ref.shape, F32)

    y_ref[...] += _swiglu_down(hf_ref[...], wg_ref, wu_ref, wd_ref)

    @pl.when(f < y_ref.shape[0] // resid)
    def _():
        rows = pl.ds(pl.multiple_of(f * resid, resid), resid)
        y_ref[rows, :] += h_ref[...]

    @pl.when(f == last)
    def _():
        y_ref[...] = _rms_rows(y_ref[...], g_ref[...])

    @pl.when(pl.program_id(0) == 0)
    def _():
        @pl.when(f == 0)
        def _():
            ys_ref[...] = hs_ref[...]

        ys_ref[...] += _swiglu_down(hfs_ref[...], wg_ref, wu_ref, wd_ref)

        @pl.when(f == last)
        def _():
            ys_ref[...] = _rms_rows(ys_ref[...], g_ref[...])


def _ffn(hf, h1, hf_s, h1_s, wg, wu, wd, g, tm, tf):
    m, ms = hf.shape[0], hf_s.shape[0]
    resid = min(RESID_ROWS, tm)
    slabs = tm // resid
    assert tm % resid == 0 and slabs <= D_FF // tf
    blocks = (tm * D_MODEL * (2 + 4) + resid * D_MODEL * 4 + 3 * D_MODEL * tf * 2
              + ms * D_MODEL * (2 + 4 + 4))
    resident = tm * D_MODEL * 4 + 3 * tm * tf * 4
    whole = lambda rows: pl.BlockSpec((rows, D_MODEL), lambda i, f: (0, 0))
    return pl.pallas_call(
        _ffn_kernel,
        grid=(m // tm, D_FF // tf),
        in_specs=[pl.BlockSpec((tm, D_MODEL), lambda i, f: (i, 0)),
                  pl.BlockSpec((resid, D_MODEL), lambda i, f: (i * slabs + jnp.minimum(f, slabs - 1), 0)),
                  whole(ms), whole(ms),
                  pl.BlockSpec((D_MODEL, tf), lambda i, f: (0, f)),
                  pl.BlockSpec((D_MODEL, tf), lambda i, f: (0, f)),
                  pl.BlockSpec((tf, D_MODEL), lambda i, f: (f, 0)),
                  pl.BlockSpec((1, D_MODEL), lambda i, f: (0, 0))],
        out_specs=[pl.BlockSpec((tm, D_MODEL), lambda i, f: (i, 0)), whole(ms)],
        out_shape=[jax.ShapeDtypeStruct((m, D_MODEL), F32),
                   jax.ShapeDtypeStruct((ms, D_MODEL), F32)],
        compiler_params=_params(_vmem_limit(blocks, resident), 2),
        name="ffn",
    )(hf, h1, hf_s, h1_s, wg, wu, wd, g)


def kernel(x_prompt, x_sample, state_hgrn, state_conv, meta_tokens, norm_mix_g, w_in, lb_logits,
           hgrn_norm_g, conv_w, conv_b, gn_g, gn_b, w_out, norm_ffn_g, w_ffn_gate, w_ffn_up,
           w_ffn_down, norm_final_g):
    batch, seq, _ = x_prompt.shape
    nb = x_sample.shape[0]
    assert x_sample.shape[1] == 1 and norm_mix_g.shape[0] == 1
    assert seq % (CHUNK * CHUNK_UNROLL) == 0 and seq % CONV_TILE == 0
    assert nb % N_META == 0 and nb % CONV_SEQS == 0 and nb % SAMPLE_ROWS == 0

    w_in_b = w_in[0].astype(BF16)
    w_out_b = w_out[0].astype(BF16)
    g_final = norm_final_g[None, :]

    xp = x_prompt.reshape(batch * seq, D_MODEL)
    xs = x_sample.reshape(nb, D_MODEL)
    x_small = jnp.concatenate([xs, meta_tokens], axis=0)

    z = _inproj(xp, norm_mix_g, w_in_b, tm=1024, tn=1024)
    z_small = _inproj(x_small, norm_mix_g, w_in_b, tm=x_small.shape[0], tn=1024)

    oa_p, s_p = _hgrn_prompt(z, z_small, lb_logits, hgrn_norm_g, batch, seq)
    ob_p, c_p, wg_b, wu_b, wd_b = _conv_prompt(z, z_small, conv_w[0], conv_b, gn_g, gn_b,
                                               w_ffn_gate[0], w_ffn_up[0], w_ffn_down[0], batch, seq)
    z_s = z_small[:nb].reshape(nb, 1, IN_COLS)
    oa_s, s_s = _hgrn_sample(z_s, state_hgrn[0], lb_logits, hgrn_norm_g)
    ob_s, c_s = _conv_sample(z_small, jnp.transpose(state_conv[0], (1, 0, 2)), conv_w[0], conv_b, gn_g, gn_b)
    c_s = jnp.transpose(c_s, (1, 0, 2))
    oa_s = oa_s.reshape(nb, A_WIDTH)

    h_p, hf_p = _outproj(oa_p, ob_p, xp, w_out_b, norm_ffn_g, tm=512)
    h_s, hf_s = _outproj(oa_s, ob_s, xs, w_out_b, norm_ffn_g, tm=nb)

    y_p, y_s = _ffn(hf_p, h_p, hf_s, h_s, wg_b, wu_b, wd_b, g_final, tm=1024, tf=512)

    return (y_p.reshape(batch, seq, D_MODEL), y_s.reshape(nb, 1, D_MODEL),
            s_p[None], c_p[None], s_s[None], c_s[None])
```

```python
import jax
import jax.numpy as jnp
from jax import lax
from jax.experimental import pallas as pl
from jax.experimental.pallas import tpu as pltpu

F32 = jnp.float32
BF16 = jnp.bfloat16

D_MODEL = 2048
N_META = 16
A_WIDTH = 1024
B_WIDTH = 1024
HEAD = 128
A_HEADS = A_WIDTH // HEAD
CONV_W = 31
HIST = CONV_W - 1
B_GROUPS = 8
GROUP = B_WIDTH // B_GROUPS
D_FF = 5632
IN_COLS = 4 * A_WIDTH + 2 * B_WIDTH
EPS = 1e-6

V7X_LANES = 128
V7X_SUBLANES = 8
V7X_VMEM_BYTES = 64 * 1024 * 1024
V7X_VMEM_USABLE = V7X_VMEM_BYTES - 8 * 1024 * 1024

CHUNK = 64
SUB = V7X_SUBLANES
NEG_BIG = -1e30
FAST_BLOCK = 16
SAFE_EXP = 60.0
HEADS_PER_STEP = 4
CHUNK_UNROLL = 4

_NT = (((1,), (1,)), ((), ()))
_TN = (((0,), (0,)), ((), ()))


def _vmem_limit(pipelined_block_bytes, resident_bytes):
    est = 2 * pipelined_block_bytes + resident_bytes
    return int(min(max(est, 16 * 1024 * 1024), V7X_VMEM_USABLE))


def _params(vmem_bytes, ndims):
    return pltpu.CompilerParams(dimension_semantics=("arbitrary",) * ndims,
                                vmem_limit_bytes=vmem_bytes)


def _sigmoid(x):
    return 1.0 / (1.0 + jnp.exp(-x))


def _silu(x):
    return x * _sigmoid(x)


def _rms_rows(x, g):
    return x * lax.rsqrt(jnp.mean(x * x, axis=-1, keepdims=True) + EPS) * g


def _inproj_kernel(x_ref, g_ref, w_ref, z_ref, xn_ref):
    @pl.when(pl.program_id(1) == 0)
    def _():
        xn_ref[...] = _rms_rows(x_ref[...], g_ref[...]).astype(BF16)

    z_ref[...] = jnp.dot(xn_ref[...], w_ref[...], preferred_element_type=F32)


def _inproj(x, g, w_bf16, tm, tn):
    m = x.shape[0]
    n = w_bf16.shape[1]
    blocks = tm * D_MODEL * 4 + D_MODEL * tn * 2 + tm * tn * 4
    resident = tm * D_MODEL * 2 + tm * tn * 4
    return pl.pallas_call(
        _inproj_kernel,
        grid=(m // tm, n // tn),
        in_specs=[pl.BlockSpec((tm, D_MODEL), lambda i, j: (i, 0)),
                  pl.BlockSpec((1, D_MODEL), lambda i, j: (0, 0)),
                  pl.BlockSpec((D_MODEL, tn), lambda i, j: (0, j))],
        out_specs=pl.BlockSpec((tm, tn), lambda i, j: (i, j)),
        out_shape=jax.ShapeDtypeStruct((m, n), F32),
        scratch_shapes=[pltpu.VMEM((tm, D_MODEL), BF16)],
        compiler_params=_params(_vmem_limit(blocks, resident), 2),
        name="inproj",
    )(x, g, w_bf16)


def _lower_bound(lb_logits):
    m = jnp.max(lb_logits, axis=0, keepdims=True)
    e = jnp.exp(lb_logits - m)
    return e[0:1, :] / jnp.sum(e, axis=0, keepdims=True)


def _cumsum_rows(x):
    n = x.shape[0]
    row = lax.broadcasted_iota(jnp.int32, x.shape, 0)
    s = 1
    while s < n:
        x = x + jnp.where(row >= s, pltpu.roll(x, s, 0), 0.0)
        s *= 2
    return x


def _gates(zf, lb):
    fg = lb + (1.0 - lb) * _sigmoid(zf)
    return 1.0 - fg, jnp.log(fg)


def _chunk_masks(c, diag_block):
    row_w = lax.broadcasted_iota(jnp.int32, (c, HEAD), 0)
    row = lax.broadcasted_iota(jnp.int32, (c, c), 0)
    col = lax.broadcasted_iota(jnp.int32, (c, c), 1)
    levels = []
    half = c // 2
    while half >= diag_block:
        blk = 2 * half
        is_right = (row_w & (blk - 1)) >= half
        same_blk = (row ^ col) < blk
        valid = same_blk & ((row & (blk - 1)) >= half) & ((col & (blk - 1)) < half)
        levels.append((half, is_right, valid))
        half //= 2
    diag = ((row ^ col) < diag_block) & (col <= row)
    return levels, diag


def _block_rows(b, blk, offset):
    return jnp.concatenate(
        [jnp.broadcast_to(b[s0 + offset:s0 + offset + 1], (blk, HEAD)) for s0 in range(0, b.shape[0], blk)],
        axis=0)


def _diag_scores_exact(q, k, b):
    c = q.shape[0]
    row8 = lax.broadcasted_iota(jnp.int32, (SUB, HEAD), 0)
    lane = lax.broadcasted_iota(jnp.int32, (SUB, c), 1)
    diag = []
    for i in range(c // SUB):
        bi, qi, ki = (t[SUB * i:SUB * (i + 1)] for t in (b, q, k))
        blk = jnp.zeros((SUB, c), F32)
        for s in range(SUB):
            arg = jnp.where(row8 >= s, bi - bi[s:s + 1], NEG_BIG)
            a = jnp.sum(qi * ki[s:s + 1] * jnp.exp(arg), axis=-1, keepdims=True)
            blk = jnp.where(lane == SUB * i + s, a, blk)
        diag.append(blk)
    return jnp.concatenate(diag, axis=0)


def _score_terms(q, k, b, masks, exact):
    levels, diag_mask = masks
    if exact:
        terms = [(None, _diag_scores_exact(q, k, b))]
    else:
        d = b - _block_rows(b, FAST_BLOCK, FAST_BLOCK // 2 - 1)
        xq = (q * jnp.exp(d)).astype(BF16)
        xk = (k * jnp.exp(-d)).astype(BF16)
        terms = [(diag_mask, lax.dot_general(xq, xk, _NT, preferred_element_type=F32))]
    for half, is_right, valid in levels:
        mid = _block_rows(b, 2 * half, half - 1)
        x = (jnp.where(is_right, q, k) * jnp.exp(-jnp.abs(b - mid))).astype(BF16)
        terms.append((valid, lax.dot_general(x, x, _NT, preferred_element_type=F32)))
    return terms


def _sum_terms(terms):
    sc = None
    for mask, p in terms:
        p = p if mask is None else jnp.where(mask, p, 0.0)
        sc = p if sc is None else sc + p
    return sc


def _chunk_state(k, v, b, st):
    b_last = b[b.shape[0] - 1:, :]
    kd = (k * jnp.exp(b_last - b)).astype(BF16)
    return st * jnp.exp(b_last) + lax.dot_general(v.astype(BF16), kd, _TN, preferred_element_type=F32)


def _hgrn_out(o, g, zog):
    return _rms_rows(o, g) * _silu(zog)


def _hgrn_prompt_kernel(zq_ref, zf_ref, zi_ref, zog_ref, mf_ref, mi_ref, lbl_ref, g_ref,
                        o_ref, s_ref):
    n_chunks = zq_ref.shape[0] // CHUNK
    heads = [slice(h * HEAD, (h + 1) * HEAD) for h in range(HEADS_PER_STEP)]
    lbs = [_lower_bound(lbl_ref[:, hs]) for hs in heads]

    def meta_state(hs, lb):
        mk, mlf = _gates(mf_ref[:, hs], lb)
        return _chunk_state(mk, mi_ref[:, hs], _cumsum_rows(mlf), jnp.zeros((HEAD, HEAD), F32))

    def scan(exact, sts):
        masks = _chunk_masks(CHUNK, SUB if exact else FAST_BLOCK)

        def body(c, sts):
            rows = pl.ds(pl.multiple_of(c * CHUNK, CHUNK), CHUNK)
            issued = []
            for hs, lb, st in zip(heads, lbs, sts):
                q = _silu(zq_ref[rows, hs])
                k, lf = _gates(zf_ref[rows, hs], lb)
                v = zi_ref[rows, hs].astype(BF16)
                b = _cumsum_rows(lf)
                o_inter = lax.dot_general((q * jnp.exp(b)).astype(BF16), st.astype(BF16), _NT,
                                          preferred_element_type=F32)
                issued.append((v, o_inter, _score_terms(q, k, b, masks, exact), _chunk_state(k, v, b, st)))
            outs = [o_inter + jnp.dot(_sum_terms(terms).astype(BF16), v, preferred_element_type=F32)
                    for v, o_inter, terms, _ in issued]
            for hs, o in zip(heads, outs):
                o_ref[rows, hs] = _hgrn_out(o, g_ref[:, hs], zog_ref[rows, hs]).astype(o_ref.dtype)
            return tuple(new_st for _, _, _, new_st in issued)

        return lax.fori_loop(0, n_chunks, body, sts, unroll=CHUNK_UNROLL)

    min_log_lb = jnp.min(jnp.log(jnp.concatenate(lbs, axis=1)))
    sts = lax.cond(min_log_lb * (FAST_BLOCK // 2) >= -SAFE_EXP,
                   lambda sts: scan(False, sts), lambda sts: scan(True, sts),
                   tuple(meta_state(hs, lb) for hs, lb in zip(heads, lbs)))
    for h, st in enumerate(sts):
        s_ref[0, h] = st.T


def _hgrn_prompt(z, z_small, lb_logits, hgrn_g, batch, seq):
    width = HEADS_PER_STEP * HEAD
    hb = A_WIDTH // width
    col = lambda base: (lambda b, h: (b, base * hb + h))
    meta_blk = z_small.shape[0] // N_META - 1
    meta = lambda base: (lambda b, h: (meta_blk, base * hb + h))
    blocks = 4 * seq * width * 4 + seq * width * 2 + 2 * N_META * width * 4 + width * HEAD * 4
    return pl.pallas_call(
        _hgrn_prompt_kernel,
        grid=(batch, hb),
        in_specs=[pl.BlockSpec((seq, width), col(0)),
                  pl.BlockSpec((seq, width), col(1)),
                  pl.BlockSpec((seq, width), col(2)),
                  pl.BlockSpec((seq, width), col(3)),
                  pl.BlockSpec((N_META, width), meta(1)),
                  pl.BlockSpec((N_META, width), meta(2)),
                  pl.BlockSpec((lb_logits.shape[0], width), lambda b, h: (0, h)),
                  pl.BlockSpec((1, width), lambda b, h: (0, h))],
        out_specs=[pl.BlockSpec((seq, width), lambda b, h: (b, h)),
                   pl.BlockSpec((1, HEADS_PER_STEP, HEAD, HEAD), lambda b, h: (b, h, 0, 0))],
        out_shape=[jax.ShapeDtypeStruct((batch * seq, A_WIDTH), BF16),
                   jax.ShapeDtypeStruct((batch, A_HEADS, HEAD, HEAD), F32)],
        compiler_params=_params(_vmem_limit(blocks, 4 * 1024 * 1024), 2),
        name="hgrn_prompt",
    )(z, z, z, z, z_small, z_small, lb_logits, hgrn_g)


SAMPLE_ROWS = 8
SAMPLE_UNROLL = 4
SPLAT_ROWS = 16


def _hgrn_sample_kernel(z_ref, s_ref, lbl_ref, g_ref, o_ref, so_ref):
    row = lax.broadcasted_iota(jnp.int32, (SPLAT_ROWS, HEAD), 0)
    ones_f = jnp.where(row < 3, 1.0, 0.0)
    ones_q = jnp.where(row == 4, 1.0, 0.0)

    def per_seq(r, carry):
        zrow = z_ref[r]
        splats = []
        for h in range(A_HEADS):
            sl = lambda base: zrow[:, base * A_WIDTH + h * HEAD: base * A_WIDTH + (h + 1) * HEAD]
            lb = _lower_bound(lbl_ref[:, h * HEAD:(h + 1) * HEAD])
            q = _silu(sl(0))
            fg = lb + (1.0 - lb) * _sigmoid(sl(1))
            k = 1.0 - fg
            v = sl(2)
            f_hi = fg.astype(BF16).astype(F32)
            f_mid = (fg - f_hi).astype(BF16).astype(F32)
            f_lo = fg - f_hi - f_mid
            a = jnp.where(row == 0, f_hi, jnp.where(row == 1, f_mid, jnp.where(
                row == 2, f_lo, jnp.where(row == 3, k, jnp.where(row == 4, q, 0.0)))))
            bmat = jnp.concatenate([ones_f, ones_q, jnp.where(row == 3, v, 0.0)], axis=1)
            splats.append(lax.dot_general(a.astype(BF16), bmat.astype(BF16), _TN,
                                          preferred_element_type=F32))
        for h, gm in enumerate(splats):
            s_new = gm[:, 0:HEAD] * s_ref[r, h] + gm[:, 2 * HEAD:3 * HEAD]
            so_ref[r, h] = s_new
            o = jnp.sum(gm[:, HEAD:2 * HEAD] * s_new, axis=0, keepdims=True)
            og = zrow[:, 3 * A_WIDTH + h * HEAD: 3 * A_WIDTH + (h + 1) * HEAD]
            o_ref[r, :, h * HEAD:(h + 1) * HEAD] = _hgrn_out(
                o, g_ref[:, h * HEAD:(h + 1) * HEAD], og)
        return carry

    lax.fori_loop(0, SAMPLE_ROWS, per_seq, 0, unroll=SAMPLE_UNROLL)


def _hgrn_sample(z_small, state, lb_logits, hgrn_g):
    nb = state.shape[0]
    blocks = SAMPLE_ROWS * (IN_COLS * 4 + A_WIDTH * 4 + 2 * A_HEADS * HEAD * HEAD * 4)
    return pl.pallas_call(
        _hgrn_sample_kernel,
        grid=(nb // SAMPLE_ROWS,),
        in_specs=[pl.BlockSpec((SAMPLE_ROWS, 1, IN_COLS), lambda i: (i, 0, 0)),
                  pl.BlockSpec((SAMPLE_ROWS, A_HEADS, HEAD, HEAD), lambda i: (i, 0, 0, 0)),
                  pl.BlockSpec(lb_logits.shape, lambda i: (0, 0)),
                  pl.BlockSpec((1, A_WIDTH), lambda i: (0, 0))],
        out_specs=[pl.BlockSpec((SAMPLE_ROWS, 1, A_WIDTH), lambda i: (i, 0, 0)),
                   pl.BlockSpec((SAMPLE_ROWS, A_HEADS, HEAD, HEAD), lambda i: (i, 0, 0, 0))],
        out_shape=[jax.ShapeDtypeStruct((nb, 1, A_WIDTH), F32),
                   jax.ShapeDtypeStruct(state.shape, F32)],
        compiler_params=_params(_vmem_limit(blocks, 2 * 1024 * 1024), 1),
        name="hgrn_sample",
    )(z_small, state, lb_logits, hgrn_g)


def _group_norm_swish(c, gn_g, gn_b):
    mu = jnp.mean(c, axis=-1, keepdims=True)
    d = c - mu
    var = jnp.mean(d * d, axis=-1, keepdims=True)
    return _silu(d * lax.rsqrt(var + EPS) * gn_g + gn_b)


CONV_PAD = 32
CONV_TILE = 256
BF16_ROWS = 2 * V7X_SUBLANES


def _conv_prompt_kernel(ga_ref, gb_ref, ma_ref, mb_ref, w_ref, cb_ref, gg_ref, gbias_ref,
                        wg_ref, wu_ref, wd_ref, o_ref, buf_ref, wg_out, wu_out, wd_out, u_ref):
    wg_out[...] = wg_ref[...].astype(BF16)
    wu_out[...] = wu_ref[...].astype(BF16)
    wd_out[...] = wd_ref[...].astype(BF16)

    seq = ga_ref.shape[0]
    u_ref[0:CONV_PAD - N_META, :] = jnp.zeros((CONV_PAD - N_META, GROUP), F32)
    u_ref[CONV_PAD - N_META:CONV_PAD, :] = ma_ref[...] * _sigmoid(mb_ref[...])
    u_ref[CONV_PAD:, :] = ga_ref[...] * _sigmoid(gb_ref[...])
    bias = cb_ref[...]
    for t0 in range(0, seq, CONV_TILE):
        acc = jnp.zeros((CONV_TILE, GROUP), F32) + bias
        for j in range(CONV_W):
            start = t0 + CONV_PAD - HIST + j
            acc = acc + w_ref[j:j + 1, :] * u_ref[start:start + CONV_TILE, :]
        o_ref[t0:t0 + CONV_TILE, :] = _group_norm_swish(acc, gg_ref[...], gbias_ref[...]).astype(
            o_ref.dtype)
    buf_ref[0] = u_ref[CONV_PAD + seq - HIST:CONV_PAD + seq, :]


def _conv_prompt(z, z_small, conv_w, conv_b, gn_g, gn_b, w_gate, w_up, w_down, batch, seq):
    ga0 = 4 * A_WIDTH // GROUP
    gb0 = ga0 + B_WIDTH // GROUP
    meta_blk = z_small.shape[0] // N_META - 1
    vec = pl.BlockSpec((1, GROUP), lambda b, g: (0, g))
    steps = batch * B_GROUPS
    assert D_MODEL % (BF16_ROWS * steps) == 0 and D_FF % (BF16_ROWS * steps) == 0
    up_rows, down_rows = D_MODEL // steps, D_FF // steps
    slab = lambda rows, cols: pl.BlockSpec((rows, cols), lambda b, g: (b * B_GROUPS + g, 0))
    blocks = (2 * seq * GROUP * 4 + seq * GROUP * 2 + 32 * GROUP * 4 * 4
              + (2 * up_rows * D_FF + down_rows * D_MODEL) * (4 + 2))
    return pl.pallas_call(
        _conv_prompt_kernel,
        grid=(batch, B_GROUPS),
        in_specs=[pl.BlockSpec((seq, GROUP), lambda b, g: (b, ga0 + g)),
                  pl.BlockSpec((seq, GROUP), lambda b, g: (b, gb0 + g)),
                  pl.BlockSpec((N_META, GROUP), lambda b, g: (meta_blk, ga0 + g)),
                  pl.BlockSpec((N_META, GROUP), lambda b, g: (meta_blk, gb0 + g)),
                  pl.BlockSpec((CONV_W, GROUP), lambda b, g: (0, g)),
                  vec, vec, vec,
                  slab(up_rows, D_FF), slab(up_rows, D_FF), slab(down_rows, D_MODEL)],
        out_specs=[pl.BlockSpec((seq, GROUP), lambda b, g: (b, g)),
                   pl.BlockSpec((1, HIST, GROUP), lambda b, g: (b, 0, g)),
                   slab(up_rows, D_FF), slab(up_rows, D_FF), slab(down_rows, D_MODEL)],
        out_shape=[jax.ShapeDtypeStruct((batch * seq, B_WIDTH), BF16),
                   jax.ShapeDtypeStruct((batch, HIST, B_WIDTH), F32),
                   jax.ShapeDtypeStruct((D_MODEL, D_FF), BF16),
                   jax.ShapeDtypeStruct((D_MODEL, D_FF), BF16),
                   jax.ShapeDtypeStruct((D_FF, D_MODEL), BF16)],
        scratch_shapes=[pltpu.VMEM((CONV_PAD + seq, GROUP), F32)],
        compiler_params=_params(_vmem_limit(blocks, (CONV_PAD + seq) * GROUP * 4 + 4 * 1024 * 1024), 2),
        name="conv_prompt",
    )(z, z, z_small, z_small, conv_w, conv_b, gn_g, gn_b, w_gate, w_up, w_down)


CONV_SEQS = 16


def _conv_sample_kernel(ga_ref, gb_ref, st_ref, w_ref, cb_ref, gg_ref, gbias_ref, o_ref, buf_ref):
    u = ga_ref[...] * _sigmoid(gb_ref[...])
    c = w_ref[HIST:CONV_W, :] * u + cb_ref[...]
    for j in range(HIST):
        c = c + w_ref[j:j + 1, :] * st_ref[j]
    for g in range(B_GROUPS):
        sl = slice(g * GROUP, (g + 1) * GROUP)
        o_ref[:, sl] = _group_norm_swish(c[:, sl], gg_ref[:, sl], gbias_ref[:, sl])
    for j in range(HIST - 1):
        buf_ref[j] = st_ref[j + 1]
    buf_ref[HIST - 1] = u


def _conv_sample(z_small, state_t, conv_w, conv_b, gn_g, gn_b):
    nb = state_t.shape[1]
    ga0 = 4 * A_WIDTH // B_WIDTH
    vec = pl.BlockSpec((1, B_WIDTH), lambda i: (0, 0))
    blocks = CONV_SEQS * B_WIDTH * 4 * (3 + 2 * HIST) + 32 * B_WIDTH * 4
    return pl.pallas_call(
        _conv_sample_kernel,
        grid=(nb // CONV_SEQS,),
        in_specs=[pl.BlockSpec((CONV_SEQS, B_WIDTH), lambda i: (i, ga0)),
                  pl.BlockSpec((CONV_SEQS, B_WIDTH), lambda i: (i, ga0 + 1)),
                  pl.BlockSpec((HIST, CONV_SEQS, B_WIDTH), lambda i: (0, i, 0)),
                  pl.BlockSpec((CONV_W, B_WIDTH), lambda i: (0, 0)),
                  vec, vec, vec],
        out_specs=[pl.BlockSpec((CONV_SEQS, B_WIDTH), lambda i: (i, 0)),
                   pl.BlockSpec((HIST, CONV_SEQS, B_WIDTH), lambda i: (0, i, 0))],
        out_shape=[jax.ShapeDtypeStruct((nb, B_WIDTH), F32),
                   jax.ShapeDtypeStruct(state_t.shape, F32)],
        compiler_params=_params(_vmem_limit(blocks, 2 * 1024 * 1024), 1),
        name="conv_sample",
    )(z_small, z_small, state_t, conv_w, conv_b, gn_g, gn_b)


def _outproj_kernel(oa_ref, ob_ref, x_ref, wa_ref, wb_ref, g_ref, h_ref, hf_ref):
    h = x_ref[...]
    h = h + jnp.dot(oa_ref[...].astype(BF16), wa_ref[...], preferred_element_type=F32)
    h = h + jnp.dot(ob_ref[...].astype(BF16), wb_ref[...], preferred_element_type=F32)
    h_ref[...] = h
    hf_ref[...] = _rms_rows(h, g_ref[...]).astype(BF16)


def _outproj(oa, ob, x, w_bf16, g, tm):
    m = x.shape[0]
    act = oa.dtype.itemsize
    blocks = 2 * tm * A_WIDTH * act + tm * D_MODEL * (4 + 4 + 2) + D_MODEL * D_MODEL * 2
    return pl.pallas_call(
        _outproj_kernel,
        grid=(m // tm,),
        in_specs=[pl.BlockSpec((tm, A_WIDTH), lambda i: (i, 0)),
                  pl.BlockSpec((tm, B_WIDTH), lambda i: (i, 0)),
                  pl.BlockSpec((tm, D_MODEL), lambda i: (i, 0)),
                  pl.BlockSpec((A_WIDTH, D_MODEL), lambda i: (0, 0)),
                  pl.BlockSpec((B_WIDTH, D_MODEL), lambda i: (1, 0)),
                  pl.BlockSpec((1, D_MODEL), lambda i: (0, 0))],
        out_specs=[pl.BlockSpec((tm, D_MODEL), lambda i: (i, 0)),
                   pl.BlockSpec((tm, D_MODEL), lambda i: (i, 0))],
        out_shape=[jax.ShapeDtypeStruct((m, D_MODEL), F32),
                   jax.ShapeDtypeStruct((m, D_MODEL), BF16)],
        compiler_params=_params(_vmem_limit(blocks, 2 * tm * D_MODEL * 4), 1),
        name="outproj",
    )(oa, ob, x, w_bf16, w_bf16, g)


RESID_ROWS = 128


def _swiglu_down(hf, wg_ref, wu_ref, wd_ref):
    gate = jnp.dot(hf, wg_ref[...], preferred_element_type=F32)
    up = jnp.dot(hf, wu_ref[...], preferred_element_type=F32)
    return jnp.dot((_silu(gate) * up).astype(BF16), wd_ref[...], preferred_element_type=F32)


def _ffn_kernel(hf_ref, h_ref, hfs_ref, hs_ref, wg_ref, wu_ref, wd_ref, g_ref, y_ref, ys_ref):
    f = pl.program_id(1)
    last = pl.num_programs(1) - 1
    resid = h_ref.shape[0]

    @pl.when(f == 0)
    def _():
        y_ref[...] = jnp.zeros(y_ref.shape, F32)

    y_ref[...] += _swiglu_down(hf_ref[...], wg_ref, wu_ref, wd_ref)

    @pl.when(f < y_ref.shape[0] // resid)
    def _():
        rows = pl.ds(pl.multiple_of(f * resid, resid), resid)
        y_ref[rows, :] += h_ref[...]

    @pl.when(f == last)
    def _():
        y_ref[...] = _rms_rows(y_ref[...], g_ref[...])

    @pl.when(pl.program_id(0) == 0)
    def _():
        @pl.when(f == 0)
        def _():
            ys_ref[...] = hs_ref[...]

        ys_ref[...] += _swiglu_down(hfs_ref[...], wg_ref, wu_ref, wd_ref)

        @pl.when(f == last)
        def _():
            ys_ref[...] = _rms_rows(ys_ref[...], g_ref[...])


def _ffn(hf, h1, hf_s, h1_s, wg, wu, wd, g, tm, tf):
    m, ms = hf.shape[0], hf_s.shape[0]
    resid = min(RESID_ROWS, tm)
    slabs = tm // resid
    assert tm % resid == 0 and slabs <= D_FF // tf
    blocks = (tm * D_MODEL * (2 + 4) + resid * D_MODEL * 4 + 3 * D_MODEL * tf * 2
              + ms * D_MODEL * (2 + 4 + 4))
    resident = tm * D_MODEL * 4 + 3 * tm * tf * 4
    whole = lambda rows: pl.BlockSpec((rows, D_MODEL), lambda i, f: (0, 0))
    return pl.pallas_call(
        _ffn_kernel,
        grid=(m // tm, D_FF // tf),
        in_specs=[pl.BlockSpec((tm, D_MODEL), lambda i, f: (i, 0)),
                  pl.BlockSpec((resid, D_MODEL), lambda i, f: (i * slabs + jnp.minimum(f, slabs - 1), 0)),
                  whole(ms), whole(ms),
                  pl.BlockSpec((D_MODEL, tf), lambda i, f: (0, f)),
                  pl.BlockSpec((D_MODEL, tf), lambda i, f: (0, f)),
                  pl.BlockSpec((tf, D_MODEL), lambda i, f: (f, 0)),
                  pl.BlockSpec((1, D_MODEL), lambda i, f: (0, 0))],
        out_specs=[pl.BlockSpec((tm, D_MODEL), lambda i, f: (i, 0)), whole(ms)],
        out_shape=[jax.ShapeDtypeStruct((m, D_MODEL), F32),
                   jax.ShapeDtypeStruct((ms, D_MODEL), F32)],
        compiler_params=_params(_vmem_limit(blocks, resident), 2),
        name="ffn",
    )(hf, h1, hf_s, h1_s, wg, wu, wd, g)


def kernel(x_prompt, x_sample, state_hgrn, state_conv, meta_tokens, norm_mix_g, w_in, lb_logits,
           hgrn_norm_g, conv_w, conv_b, gn_g, gn_b, w_out, norm_ffn_g, w_ffn_gate, w_ffn_up,
           w_ffn_down, norm_final_g):
    batch, seq, _ = x_prompt.shape
    nb = x_sample.shape[0]
    assert x_sample.shape[1] == 1 and norm_mix_g.shape[0] == 1
    assert seq % (CHUNK * CHUNK_UNROLL) == 0 and seq % CONV_TILE == 0
    assert nb % N_META == 0 and nb % CONV_SEQS == 0 and nb % SAMPLE_ROWS == 0

    w_in_b = w_in[0].astype(BF16)
    w_out_b = w_out[0].astype(BF16)
    g_final = norm_final_g[None, :]

    xp = x_prompt.reshape(batch * seq, D_MODEL)
    xs = x_sample.reshape(nb, D_MODEL)
    x_small = jnp.concatenate([xs, meta_tokens], axis=0)

    z = _inproj(xp, norm_mix_g, w_in_b, tm=1024, tn=1024)
    z_small = _inproj(x_small, norm_mix_g, w_in_b, tm=x_small.shape[0], tn=1024)

    oa_p, s_p = _hgrn_prompt(z, z_small, lb_logits, hgrn_norm_g, batch, seq)
    ob_p, c_p, wg_b, wu_b, wd_b = _conv_prompt(z, z_small, conv_w[0], conv_b, gn_g, gn_b,
                                               w_ffn_gate[0], w_ffn_up[0], w_ffn_down[0], batch, seq)
    z_s = z_small[:nb].reshape(nb, 1, IN_COLS)
    oa_s, s_s = _hgrn_sample(z_s, state_hgrn[0], lb_logits, hgrn_norm_g)
    ob_s, c_s = _conv_sample(z_small, jnp.transpose(state_conv[0], (1, 0, 2)), conv_w[0], conv_b, gn_g, gn_b)
    c_s = jnp.transpose(c_s, (1, 0, 2))
    oa_s = oa_s.reshape(nb, A_WIDTH)

    h_p, hf_p = _outproj(oa_p, ob_p, xp, w_out_b, norm_ffn_g, tm=512)
    h_s, hf_s = _outproj(oa_s, ob_s, xs, w_out_b, norm_ffn_g, tm=nb)

    y_p, y_s = _ffn(hf_p, h_p, hf_s, h_s, wg_b, wu_b, wd_b, g_final, tm=1024, tf=512)

    return (y_p.reshape(batch, seq, D_MODEL), y_s.reshape(nb, 1, D_MODEL),
            s_p[None], c_p[None], s_s[None], c_s[None])
```

```python
import jax
import jax.numpy as jnp
from jax import lax
from jax.experimental import pallas as pl
from jax.experimental.pallas import tpu as pltpu

F32 = jnp.float32
BF16 = jnp.bfloat16

D_MODEL = 2048
N_META = 16
A_WIDTH = 1024
B_WIDTH = 1024
HEAD = 128
A_HEADS = A_WIDTH // HEAD
CONV_W = 31
HIST = CONV_W - 1
B_GROUPS = 8
GROUP = B_WIDTH // B_GROUPS
D_FF = 5632
IN_COLS = 4 * A_WIDTH + 2 * B_WIDTH
EPS = 1e-6

V7X_LANES = 128
V7X_SUBLANES = 8
V7X_VMEM_BYTES = 64 * 1024 * 1024
V7X_VMEM_USABLE = V7X_VMEM_BYTES - 8 * 1024 * 1024
BF16_ROWS = 2 * V7X_SUBLANES

CHUNK = 64
SUB = V7X_SUBLANES
NEG_BIG = -1e30
FAST_BLOCK = 16
SAFE_EXP = 60.0
HEADS_PER_STEP = 4
CHUNK_UNROLL = 4

_NT = (((1,), (1,)), ((), ()))
_TN = (((0,), (0,)), ((), ()))


def _vmem_limit(pipelined_block_bytes, resident_bytes):
    est = 2 * pipelined_block_bytes + resident_bytes
    return int(min(max(est, 16 * 1024 * 1024), V7X_VMEM_USABLE))


def _params(vmem_bytes, ndims):
    return pltpu.CompilerParams(dimension_semantics=("arbitrary",) * ndims,
                                vmem_limit_bytes=vmem_bytes)


def _sigmoid(x):
    return 1.0 / (1.0 + jnp.exp(-x))


def _silu(x):
    return x * _sigmoid(x)


def _rms_rows(x, g):
    return x * lax.rsqrt(jnp.mean(x * x, axis=-1, keepdims=True) + EPS) * g


def _inproj_kernel(x_ref, xs_ref, g_ref, w_ref, z_ref, zs_ref, xn_ref, xsn_ref):
    i, j = pl.program_id(0), pl.program_id(1)

    @pl.when(j == 0)
    def _():
        xn_ref[...] = _rms_rows(x_ref[...], g_ref[...]).astype(BF16)

    z_ref[...] = jnp.dot(xn_ref[...], w_ref[...], preferred_element_type=F32)

    @pl.when(i == 0)
    def _():
        @pl.when(j == 0)
        def _():
            xsn_ref[...] = _rms_rows(xs_ref[...], g_ref[...]).astype(BF16)

        zs_ref[...] = jnp.dot(xsn_ref[...], w_ref[...], preferred_element_type=F32)


def _inproj(x, x_small, g, w_bf16, tm, tn):
    m, ms = x.shape[0], x_small.shape[0]
    n = w_bf16.shape[1]
    nj = n // tn
    blocks = (tm + ms) * D_MODEL * 4 + D_MODEL * tn * 2 + (tm + ms) * tn * 4
    resident = (tm + ms) * D_MODEL * 2 + tm * tn * 4
    return pl.pallas_call(
        _inproj_kernel,
        grid=(m // tm, nj),
        in_specs=[pl.BlockSpec((tm, D_MODEL), lambda i, j: (i, 0)),
                  pl.BlockSpec((ms, D_MODEL), lambda i, j: (0, 0)),
                  pl.BlockSpec((1, D_MODEL), lambda i, j: (0, 0)),
                  pl.BlockSpec((D_MODEL, tn), lambda i, j: (0, j))],
        out_specs=[pl.BlockSpec((tm, tn), lambda i, j: (i, j)),
                   pl.BlockSpec((ms, tn), lambda i, j: (0, jnp.where(i == 0, j, nj - 1)))],
        out_shape=[jax.ShapeDtypeStruct((m, n), F32), jax.ShapeDtypeStruct((ms, n), F32)],
        scratch_shapes=[pltpu.VMEM((tm, D_MODEL), BF16), pltpu.VMEM((ms, D_MODEL), BF16)],
        compiler_params=_params(_vmem_limit(blocks, resident), 2),
        name="inproj",
    )(x, x_small, g, w_bf16)


def _lower_bound(lb_logits):
    m = jnp.max(lb_logits, axis=0, keepdims=True)
    e = jnp.exp(lb_logits - m)
    return e[0:1, :] / jnp.sum(e, axis=0, keepdims=True)


def _cumsum_rows(x):
    n = x.shape[0]
    row = lax.broadcasted_iota(jnp.int32, x.shape, 0)
    s = 1
    while s < n:
        x = x + jnp.where(row >= s, pltpu.roll(x, s, 0), 0.0)
        s *= 2
    return x


def _gates(zf, lb):
    fg = lb + (1.0 - lb) * _sigmoid(zf)
    return 1.0 - fg, jnp.log(fg)


def _chunk_masks(c, diag_block):
    row_w = lax.broadcasted_iota(jnp.int32, (c, HEAD), 0)
    row = lax.broadcasted_iota(jnp.int32, (c, c), 0)
    col = lax.broadcasted_iota(jnp.int32, (c, c), 1)
    levels = []
    half = c // 2
    while half >= diag_block:
        blk = 2 * half
        is_right = (row_w & (blk - 1)) >= half
        same_blk = (row ^ col) < blk
        valid = same_blk & ((row & (blk - 1)) >= half) & ((col & (blk - 1)) < half)
        levels.append((half, is_right, valid))
        half //= 2
    diag = ((row ^ col) < diag_block) & (col <= row)
    return levels, diag


def _block_rows(b, blk, offset):
    return jnp.concatenate(
        [jnp.broadcast_to(b[s0 + offset:s0 + offset + 1], (blk, HEAD)) for s0 in range(0, b.shape[0], blk)],
        axis=0)


def _diag_scores_exact(q, k, b):
    c = q.shape[0]
    row8 = lax.broadcasted_iota(jnp.int32, (SUB, HEAD), 0)
    lane = lax.broadcasted_iota(jnp.int32, (SUB, c), 1)
    diag = []
    for i in range(c // SUB):
        bi, qi, ki = (t[SUB * i:SUB * (i + 1)] for t in (b, q, k))
        blk = jnp.zeros((SUB, c), F32)
        for s in range(SUB):
            arg = jnp.where(row8 >= s, bi - bi[s:s + 1], NEG_BIG)
            a = jnp.sum(qi * ki[s:s + 1] * jnp.exp(arg), axis=-1, keepdims=True)
            blk = jnp.where(lane == SUB * i + s, a, blk)
        diag.append(blk)
    return jnp.concatenate(diag, axis=0)


def _score_terms(q, k, b, masks, exact):
    levels, diag_mask = masks
    if exact:
        terms = [(None, _diag_scores_exact(q, k, b))]
    else:
        d = b - _block_rows(b, FAST_BLOCK, FAST_BLOCK // 2 - 1)
        xq = (q * jnp.exp(d)).astype(BF16)
        xk = (k * jnp.exp(-d)).astype(BF16)
        terms = [(diag_mask, lax.dot_general(xq, xk, _NT, preferred_element_type=F32))]
    for half, is_right, valid in levels:
        mid = _block_rows(b, 2 * half, half - 1)
        x = (jnp.where(is_right, q, k) * jnp.exp(-jnp.abs(b - mid))).astype(BF16)
        terms.append((valid, lax.dot_general(x, x, _NT, preferred_element_type=F32)))
    return terms


def _sum_terms(terms):
    sc = None
    for mask, p in terms:
        p = p if mask is None else jnp.where(mask, p, 0.0)
        sc = p if sc is None else sc + p
    return sc


def _chunk_state(k, v, b, st):
    b_last = b[b.shape[0] - 1:, :]
    kd = (k * jnp.exp(b_last - b)).astype(BF16)
    return st * jnp.exp(b_last) + lax.dot_general(v.astype(BF16), kd, _TN, preferred_element_type=F32)


def _hgrn_out(o, g, zog):
    return _rms_rows(o, g) * _silu(zog)


def _hgrn_prompt_kernel(zq_ref, zf_ref, zi_ref, zog_ref, mf_ref, mi_ref, lbl_ref, g_ref, wo_ref,
                        o_ref, s_ref, wo_out):
    wo_out[...] = wo_ref[...].astype(BF16)

    n_chunks = zq_ref.shape[0] // CHUNK
    heads = [slice(h * HEAD, (h + 1) * HEAD) for h in range(HEADS_PER_STEP)]
    lbs = [_lower_bound(lbl_ref[:, hs]) for hs in heads]

    def meta_state(hs, lb):
        mk, mlf = _gates(mf_ref[:, hs], lb)
        return _chunk_state(mk, mi_ref[:, hs], _cumsum_rows(mlf), jnp.zeros((HEAD, HEAD), F32))

    def scan(exact, sts):
        masks = _chunk_masks(CHUNK, SUB if exact else FAST_BLOCK)

        def body(c, sts):
            rows = pl.ds(pl.multiple_of(c * CHUNK, CHUNK), CHUNK)
            issued = []
            for hs, lb, st in zip(heads, lbs, sts):
                q = _silu(zq_ref[rows, hs])
                k, lf = _gates(zf_ref[rows, hs], lb)
                v = zi_ref[rows, hs].astype(BF16)
                b = _cumsum_rows(lf)
                o_inter = lax.dot_general((q * jnp.exp(b)).astype(BF16), st.astype(BF16), _NT,
                                          preferred_element_type=F32)
                issued.append((v, o_inter, _score_terms(q, k, b, masks, exact), _chunk_state(k, v, b, st)))
            outs = [o_inter + jnp.dot(_sum_terms(terms).astype(BF16), v, preferred_element_type=F32)
                    for v, o_inter, terms, _ in issued]
            for hs, o in zip(heads, outs):
                o_ref[rows, hs] = _hgrn_out(o, g_ref[:, hs], zog_ref[rows, hs]).astype(o_ref.dtype)
            return tuple(new_st for _, _, _, new_st in issued)

        return lax.fori_loop(0, n_chunks, body, sts, unroll=CHUNK_UNROLL)

    min_log_lb = jnp.min(jnp.log(jnp.concatenate(lbs, axis=1)))
    sts = lax.cond(min_log_lb * (FAST_BLOCK // 2) >= -SAFE_EXP,
                   lambda sts: scan(False, sts), lambda sts: scan(True, sts),
                   tuple(meta_state(hs, lb) for hs, lb in zip(heads, lbs)))
    for h, st in enumerate(sts):
        s_ref[0, h] = st.T


def _hgrn_prompt(z, z_small, lb_logits, hgrn_g, w_out, batch, seq):
    width = HEADS_PER_STEP * HEAD
    hb = A_WIDTH // width
    steps = batch * hb
    assert w_out.shape[0] % (BF16_ROWS * steps) == 0
    wo_rows = w_out.shape[0] // steps
    wo_slab = pl.BlockSpec((wo_rows, D_MODEL), lambda b, h: (b * hb + h, 0))
    col = lambda base: (lambda b, h: (b, base * hb + h))
    meta_blk = z_small.shape[0] // N_META - 1
    meta = lambda base: (lambda b, h: (meta_blk, base * hb + h))
    blocks = (4 * seq * width * 4 + seq * width * 2 + 2 * N_META * width * 4 + width * HEAD * 4
              + wo_rows * D_MODEL * (4 + 2))
    return pl.pallas_call(
        _hgrn_prompt_kernel,
        grid=(batch, hb),
        in_specs=[pl.BlockSpec((seq, width), col(0)),
                  pl.BlockSpec((seq, width), col(1)),
                  pl.BlockSpec((seq, width), col(2)),
                  pl.BlockSpec((seq, width), col(3)),
                  pl.BlockSpec((N_META, width), meta(1)),
                  pl.BlockSpec((N_META, width), meta(2)),
                  pl.BlockSpec((lb_logits.shape[0], width), lambda b, h: (0, h)),
                  pl.BlockSpec((1, width), lambda b, h: (0, h)),
                  wo_slab],
        out_specs=[pl.BlockSpec((seq, width), lambda b, h: (b, h)),
                   pl.BlockSpec((1, HEADS_PER_STEP, HEAD, HEAD), lambda b, h: (b, h, 0, 0)),
                   wo_slab],
        out_shape=[jax.ShapeDtypeStruct((batch * seq, A_WIDTH), BF16),
                   jax.ShapeDtypeStruct((batch, A_HEADS, HEAD, HEAD), F32),
                   jax.ShapeDtypeStruct(w_out.shape, BF16)],
        compiler_params=_params(_vmem_limit(blocks, 4 * 1024 * 1024), 2),
        name="hgrn_prompt",
    )(z, z, z, z, z_small, z_small, lb_logits, hgrn_g, w_out)


SAMPLE_ROWS = 8
SAMPLE_UNROLL = 4
SPLAT_ROWS = 16


def _hgrn_sample_kernel(z_ref, s_ref, lbl_ref, g_ref, o_ref, so_ref):
    row = lax.broadcasted_iota(jnp.int32, (SPLAT_ROWS, HEAD), 0)
    ones_f = jnp.where(row < 3, 1.0, 0.0)
    ones_q = jnp.where(row == 4, 1.0, 0.0)

    def per_seq(r, carry):
        zrow = z_ref[r]
        splats = []
        for h in range(A_HEADS):
            sl = lambda base: zrow[:, base * A_WIDTH + h * HEAD: base * A_WIDTH + (h + 1) * HEAD]
            lb = _lower_bound(lbl_ref[:, h * HEAD:(h + 1) * HEAD])
            q = _silu(sl(0))
            fg = lb + (1.0 - lb) * _sigmoid(sl(1))
            k = 1.0 - fg
            v = sl(2)
            f_hi = fg.astype(BF16).astype(F32)
            f_mid = (fg - f_hi).astype(BF16).astype(F32)
            f_lo = fg - f_hi - f_mid
            a = jnp.where(row == 0, f_hi, jnp.where(row == 1, f_mid, jnp.where(
                row == 2, f_lo, jnp.where(row == 3, k, jnp.where(row == 4, q, 0.0)))))
            bmat = jnp.concatenate([ones_f, ones_q, jnp.where(row == 3, v, 0.0)], axis=1)
            splats.append(lax.dot_general(a.astype(BF16), bmat.astype(BF16), _TN,
                                          preferred_element_type=F32))
        for h, gm in enumerate(splats):
            s_new = gm[:, 0:HEAD] * s_ref[r, h] + gm[:, 2 * HEAD:3 * HEAD]
            so_ref[r, h] = s_new
            o = jnp.sum(gm[:, HEAD:2 * HEAD] * s_new, axis=0, keepdims=True)
            og = zrow[:, 3 * A_WIDTH + h * HEAD: 3 * A_WIDTH + (h + 1) * HEAD]
            o_ref[r, :, h * HEAD:(h + 1) * HEAD] = _hgrn_out(
                o, g_ref[:, h * HEAD:(h + 1) * HEAD], og)
        return carry

    lax.fori_loop(0, SAMPLE_ROWS, per_seq, 0, unroll=SAMPLE_UNROLL)


def _hgrn_sample(z_small, state, lb_logits, hgrn_g):
    nb = state.shape[0]
    blocks = SAMPLE_ROWS * (IN_COLS * 4 + A_WIDTH * 4 + 2 * A_HEADS * HEAD * HEAD * 4)
    return pl.pallas_call(
        _hgrn_sample_kernel,
        grid=(nb // SAMPLE_ROWS,),
        in_specs=[pl.BlockSpec((SAMPLE_ROWS, 1, IN_COLS), lambda i: (i, 0, 0)),
                  pl.BlockSpec((SAMPLE_ROWS, A_HEADS, HEAD, HEAD), lambda i: (i, 0, 0, 0)),
                  pl.BlockSpec(lb_logits.shape, lambda i: (0, 0)),
                  pl.BlockSpec((1, A_WIDTH), lambda i: (0, 0))],
        out_specs=[pl.BlockSpec((SAMPLE_ROWS, 1, A_WIDTH), lambda i: (i, 0, 0)),
                   pl.BlockSpec((SAMPLE_ROWS, A_HEADS, HEAD, HEAD), lambda i: (i, 0, 0, 0))],
        out_shape=[jax.ShapeDtypeStruct((nb, 1, A_WIDTH), F32),
                   jax.ShapeDtypeStruct(state.shape, F32)],
        compiler_params=_params(_vmem_limit(blocks, 2 * 1024 * 1024), 1),
        name="hgrn_sample",
    )(z_small, state, lb_logits, hgrn_g)


def _group_norm_swish(c, gn_g, gn_b):
    mu = jnp.mean(c, axis=-1, keepdims=True)
    d = c - mu
    var = jnp.mean(d * d, axis=-1, keepdims=True)
    return _silu(d * lax.rsqrt(var + EPS) * gn_g + gn_b)


CONV_PAD = 32
CONV_TILE = 256


def _conv_prompt_kernel(ga_ref, gb_ref, ma_ref, mb_ref, w_ref, cb_ref, gg_ref, gbias_ref,
                        wg_ref, wu_ref, wd_ref, o_ref, buf_ref, wg_out, wu_out, wd_out, u_ref):
    wg_out[...] = wg_ref[...].astype(BF16)
    wu_out[...] = wu_ref[...].astype(BF16)
    wd_out[...] = wd_ref[...].astype(BF16)

    seq = ga_ref.shape[0]
    u_ref[0:CONV_PAD - N_META, :] = jnp.zeros((CONV_PAD - N_META, GROUP), F32)
    u_ref[CONV_PAD - N_META:CONV_PAD, :] = ma_ref[...] * _sigmoid(mb_ref[...])
    u_ref[CONV_PAD:, :] = ga_ref[...] * _sigmoid(gb_ref[...])
    bias = cb_ref[...]
    for t0 in range(0, seq, CONV_TILE):
        acc = jnp.zeros((CONV_TILE, GROUP), F32) + bias
        for j in range(CONV_W):
            start = t0 + CONV_PAD - HIST + j
            acc = acc + w_ref[j:j + 1, :] * u_ref[start:start + CONV_TILE, :]
        o_ref[t0:t0 + CONV_TILE, :] = _group_norm_swish(acc, gg_ref[...], gbias_ref[...]).astype(
            o_ref.dtype)
    buf_ref[0] = u_ref[CONV_PAD + seq - HIST:CONV_PAD + seq, :]


def _conv_prompt(z, z_small, conv_w, conv_b, gn_g, gn_b, w_gate, w_up, w_down, batch, seq):
    ga0 = 4 * A_WIDTH // GROUP
    gb0 = ga0 + B_WIDTH // GROUP
    meta_blk = z_small.shape[0] // N_META - 1
    vec = pl.BlockSpec((1, GROUP), lambda b, g: (0, g))
    steps = batch * B_GROUPS
    assert D_MODEL % (BF16_ROWS * steps) == 0 and D_FF % (BF16_ROWS * steps) == 0
    up_rows, down_rows = D_MODEL // steps, D_FF // steps
    slab = lambda rows, cols: pl.BlockSpec((rows, cols), lambda b, g: (b * B_GROUPS + g, 0))
    blocks = (2 * seq * GROUP * 4 + seq * GROUP * 2 + 32 * GROUP * 4 * 4
              + (2 * up_rows * D_FF + down_rows * D_MODEL) * (4 + 2))
    return pl.pallas_call(
        _conv_prompt_kernel,
        grid=(batch, B_GROUPS),
        in_specs=[pl.BlockSpec((seq, GROUP), lambda b, g: (b, ga0 + g)),
                  pl.BlockSpec((seq, GROUP), lambda b, g: (b, gb0 + g)),
                  pl.BlockSpec((N_META, GROUP), lambda b, g: (meta_blk, ga0 + g)),
                  pl.BlockSpec((N_META, GROUP), lambda b, g: (meta_blk, gb0 + g)),
                  pl.BlockSpec((CONV_W, GROUP), lambda b, g: (0, g)),
                  vec, vec, vec,
                  slab(up_rows, D_FF), slab(up_rows, D_FF), slab(down_rows, D_MODEL)],
        out_specs=[pl.BlockSpec((seq, GROUP), lambda b, g: (b, g)),
                   pl.BlockSpec((1, HIST, GROUP), lambda b, g: (b, 0, g)),
                   slab(up_rows, D_FF), slab(up_rows, D_FF), slab(down_rows, D_MODEL)],
        out_shape=[jax.ShapeDtypeStruct((batch * seq, B_WIDTH), BF16),
                   jax.ShapeDtypeStruct((batch, HIST, B_WIDTH), F32),
                   jax.ShapeDtypeStruct((D_MODEL, D_FF), BF16),
                   jax.ShapeDtypeStruct((D_MODEL, D_FF), BF16),
                   jax.ShapeDtypeStruct((D_FF, D_MODEL), BF16)],
        scratch_shapes=[pltpu.VMEM((CONV_PAD + seq, GROUP), F32)],
        compiler_params=_params(_vmem_limit(blocks, (CONV_PAD + seq) * GROUP * 4 + 4 * 1024 * 1024), 2),
        name="conv_prompt",
    )(z, z, z_small, z_small, conv_w, conv_b, gn_g, gn_b, w_gate, w_up, w_down)


CONV_SEQS = 16


def _conv_sample_kernel(ga_ref, gb_ref, st_ref, w_ref, cb_ref, gg_ref, gbias_ref, o_ref, buf_ref):
    u = ga_ref[...] * _sigmoid(gb_ref[...])
    c = w_ref[HIST:CONV_W, :] * u + cb_ref[...]
    for j in range(HIST):
        c = c + w_ref[j:j + 1, :] * st_ref[j]
    for g in range(B_GROUPS):
        sl = slice(g * GROUP, (g + 1) * GROUP)
        o_ref[:, sl] = _group_norm_swish(c[:, sl], gg_ref[:, sl], gbias_ref[:, sl])
    for j in range(HIST - 1):
        buf_ref[j] = st_ref[j + 1]
    buf_ref[HIST - 1] = u


def _conv_sample(z_small, state_t, conv_w, conv_b, gn_g, gn_b):
    nb = state_t.shape[1]
    ga0 = 4 * A_WIDTH // B_WIDTH
    vec = pl.BlockSpec((1, B_WIDTH), lambda i: (0, 0))
    blocks = CONV_SEQS * B_WIDTH * 4 * (3 + 2 * HIST) + 32 * B_WIDTH * 4
    return pl.pallas_call(
        _conv_sample_kernel,
        grid=(nb // CONV_SEQS,),
        in_specs=[pl.BlockSpec((CONV_SEQS, B_WIDTH), lambda i: (i, ga0)),
                  pl.BlockSpec((CONV_SEQS, B_WIDTH), lambda i: (i, ga0 + 1)),
                  pl.BlockSpec((HIST, CONV_SEQS, B_WIDTH), lambda i: (0, i, 0)),
                  pl.BlockSpec((CONV_W, B_WIDTH), lambda i: (0, 0)),
                  vec, vec, vec],
        out_specs=[pl.BlockSpec((CONV_SEQS, B_WIDTH), lambda i: (i, 0)),
                   pl.BlockSpec((HIST, CONV_SEQS, B_WIDTH), lambda i: (0, i, 0))],
        out_shape=[jax.ShapeDtypeStruct((nb, B_WIDTH), F32),
                   jax.ShapeDtypeStruct(state_t.shape, F32)],
        compiler_params=_params(_vmem_limit(blocks, 2 * 1024 * 1024), 1),
        name="conv_sample",
    )(z_small, z_small, state_t, conv_w, conv_b, gn_g, gn_b)


def _outproj_kernel(oa_ref, ob_ref, x_ref, wa_ref, wb_ref, g_ref, h_ref, hf_ref):
    h = x_ref[...]
    h = h + jnp.dot(oa_ref[...].astype(BF16), wa_ref[...], preferred_element_type=F32)
    h = h + jnp.dot(ob_ref[...].astype(BF16), wb_ref[...], preferred_element_type=F32)
    h_ref[...] = h
    hf_ref[...] = _rms_rows(h, g_ref[...]).astype(BF16)


def _outproj(oa, ob, x, w_bf16, g, tm):
    m = x.shape[0]
    act = oa.dtype.itemsize
    blocks = 2 * tm * A_WIDTH * act + tm * D_MODEL * (4 + 4 + 2) + D_MODEL * D_MODEL * 2
    return pl.pallas_call(
        _outproj_kernel,
        grid=(m // tm,),
        in_specs=[pl.BlockSpec((tm, A_WIDTH), lambda i: (i, 0)),
                  pl.BlockSpec((tm, B_WIDTH), lambda i: (i, 0)),
                  pl.BlockSpec((tm, D_MODEL), lambda i: (i, 0)),
                  pl.BlockSpec((A_WIDTH, D_MODEL), lambda i: (0, 0)),
                  pl.BlockSpec((B_WIDTH, D_MODEL), lambda i: (1, 0)),
                  pl.BlockSpec((1, D_MODEL), lambda i: (0, 0))],
        out_specs=[pl.BlockSpec((tm, D_MODEL), lambda i: (i, 0)),
                   pl.BlockSpec((tm, D_MODEL), lambda i: (i, 0))],
        out_shape=[jax.ShapeDtypeStruct((m, D_MODEL), F32),
                   jax.ShapeDtypeStruct((m, D_MODEL), BF16)],
        compiler_params=_params(_vmem_limit(blocks, 2 * tm * D_MODEL * 4), 1),
        name="outproj",
    )(oa, ob, x, w_bf16, w_bf16, g)


RESID_ROWS = 128


def _swiglu_down(hf, wg_ref, wu_ref, wd_ref):
    gate = jnp.dot(hf, wg_ref[...], preferred_element_type=F32)
    up = jnp.dot(hf, wu_ref[...], preferred_element_type=F32)
    return jnp.dot((_silu(gate) * up).astype(BF16), wd_ref[...], preferred_element_type=F32)


def _ffn_kernel(hf_ref, h_ref, hfs_ref, hs_ref, wg_ref, wu_ref, wd_ref, g_ref, y_ref, ys_ref):
    f = pl.program_id(1)
    last = pl.num_programs(1) - 1
    resid = h_ref.shape[0]

    @pl.when(f == 0)
    def _():
        y_ref[...] = jnp.zeros(y_ref.shape, F32)

    y_ref[...] += _swiglu_down(hf_ref[...], wg_ref, wu_ref, wd_ref)

    @pl.when(f < y_ref.shape[0] // resid)
    def _():
        rows = pl.ds(pl.multiple_of(f * resid, resid), resid)
        y_ref[rows, :] += h_ref[...]

    @pl.when(f == last)
    def _():
        y_ref[...] = _rms_rows(y_ref[...], g_ref[...])

    @pl.when(pl.program_id(0) == 0)
    def _():
        @pl.when(f == 0)
        def _():
            ys_ref[...] = hs_ref[...]

        ys_ref[...] += _swiglu_down(hfs_ref[...], wg_ref, wu_ref, wd_ref)

        @pl.when(f == last)
        def _():
            ys_ref[...] = _rms_rows(ys_ref[...], g_ref[...])


def _ffn(hf, h1, hf_s, h1_s, wg, wu, wd, g, tm, tf):
    m, ms = hf.shape[0], hf_s.shape[0]
    resid = min(RESID_ROWS, tm)
    slabs = tm // resid
    assert tm % resid == 0 and slabs <= D_FF // tf
    blocks = (tm * D_MODEL * (2 + 4) + resid * D_MODEL * 4 + 3 * D_MODEL * tf * 2
              + ms * D_MODEL * (2 + 4 + 4))
    resident = tm * D_MODEL * 4 + 3 * tm * tf * 4
    whole = lambda rows: pl.BlockSpec((rows, D_MODEL), lambda i, f: (0, 0))
    return pl.pallas_call(
        _ffn_kernel,
        grid=(m // tm, D_FF // tf),
        in_specs=[pl.BlockSpec((tm, D_MODEL), lambda i, f: (i, 0)),
                  pl.BlockSpec((resid, D_MODEL), lambda i, f: (i * slabs + jnp.minimum(f, slabs - 1), 0)),
                  whole(ms), whole(ms),
                  pl.BlockSpec((D_MODEL, tf), lambda i, f: (0, f)),
                  pl.BlockSpec((D_MODEL, tf), lambda i, f: (0, f)),
                  pl.BlockSpec((tf, D_MODEL), lambda i, f: (f, 0)),
                  pl.BlockSpec((1, D_MODEL), lambda i, f: (0, 0))],
        out_specs=[pl.BlockSpec((tm, D_MODEL), lambda i, f: (i, 0)), whole(ms)],
        out_shape=[jax.ShapeDtypeStruct((m, D_MODEL), F32),
                   jax.ShapeDtypeStruct((ms, D_MODEL), F32)],
        compiler_params=_params(_vmem_limit(blocks, resident), 2),
        name="ffn",
    )(hf, h1, hf_s, h1_s, wg, wu, wd, g)


def kernel(x_prompt, x_sample, state_hgrn, state_conv, meta_tokens, norm_mix_g, w_in, lb_logits,
           hgrn_norm_g, conv_w, conv_b, gn_g, gn_b, w_out, norm_ffn_g, w_ffn_gate, w_ffn_up,
           w_ffn_down, norm_final_g):
    batch, seq, _ = x_prompt.shape
    nb = x_sample.shape[0]
    assert x_sample.shape[1] == 1 and norm_mix_g.shape[0] == 1
    assert seq % (CHUNK * CHUNK_UNROLL) == 0 and seq % CONV_TILE == 0
    assert nb % N_META == 0 and nb % CONV_SEQS == 0 and nb % SAMPLE_ROWS == 0

    w_in_b = w_in[0].astype(BF16)
    g_final = norm_final_g[None, :]

    xp = x_prompt.reshape(batch * seq, D_MODEL)
    xs = x_sample.reshape(nb, D_MODEL)
    x_small = jnp.concatenate([xs, meta_tokens], axis=0)

    z, z_small = _inproj(xp, x_small, norm_mix_g, w_in_b, tm=1024, tn=1536)

    oa_p, s_p, w_out_b = _hgrn_prompt(z, z_small, lb_logits, hgrn_norm_g, w_out[0], batch, seq)
    ob_p, c_p, wg_b, wu_b, wd_b = _conv_prompt(z, z_small, conv_w[0], conv_b, gn_g, gn_b,
                                               w_ffn_gate[0], w_ffn_up[0], w_ffn_down[0], batch, seq)
    z_s = z_small[:nb].reshape(nb, 1, IN_COLS)
    oa_s, s_s = _hgrn_sample(z_s, state_hgrn[0], lb_logits, hgrn_norm_g)
    ob_s, c_s = _conv_sample(z_small, jnp.transpose(state_conv[0], (1, 0, 2)), conv_w[0], conv_b, gn_g, gn_b)
    c_s = jnp.transpose(c_s, (1, 0, 2))
    oa_s = oa_s.reshape(nb, A_WIDTH)

    h_p, hf_p = _outproj(oa_p, ob_p, xp, w_out_b, norm_ffn_g, tm=512)
    h_s, hf_s = _outproj(oa_s, ob_s, xs, w_out_b, norm_ffn_g, tm=nb)

    y_p, y_s = _ffn(hf_p, h_p, hf_s, h_s, wg_b, wu_b, wd_b, g_final, tm=1024, tf=512)

    return (y_p.reshape(batch, seq, D_MODEL), y_s.reshape(nb, 1, D_MODEL),
            s_p[None], c_p[None], s_s[None], c_s[None])
```

```python
import jax
import jax.numpy as jnp
from jax import lax
from jax.experimental import pallas as pl
from jax.experimental.pallas import tpu as pltpu

F32 = jnp.float32
BF16 = jnp.bfloat16

D_MODEL = 2048
N_META = 16
A_WIDTH = 1024
B_WIDTH = 1024
HEAD = 128
A_HEADS = A_WIDTH // HEAD
CONV_W = 31
HIST = CONV_W - 1
B_GROUPS = 8
GROUP = B_WIDTH // B_GROUPS
D_FF = 5632
IN_COLS = 4 * A_WIDTH + 2 * B_WIDTH
EPS = 1e-6

V7X_LANES = 128
V7X_SUBLANES = 8
V7X_VMEM_BYTES = 64 * 1024 * 1024
V7X_VMEM_USABLE = V7X_VMEM_BYTES - 8 * 1024 * 1024
BF16_ROWS = 2 * V7X_SUBLANES

CHUNK = 64
SUB = V7X_SUBLANES
NEG_BIG = -1e30
FAST_BLOCK = 32
SAFE_EXP = 60.0
HEADS_PER_STEP = 4
CHUNK_UNROLL = 4

_NT = (((1,), (1,)), ((), ()))
_TN = (((0,), (0,)), ((), ()))


def _vmem_limit(pipelined_block_bytes, resident_bytes):
    est = 2 * pipelined_block_bytes + resident_bytes
    return int(min(max(est, 16 * 1024 * 1024), V7X_VMEM_USABLE))


def _params(vmem_bytes, ndims):
    return pltpu.CompilerParams(dimension_semantics=("arbitrary",) * ndims,
                                vmem_limit_bytes=vmem_bytes)


def _sigmoid(x):
    return 1.0 / (1.0 + jnp.exp(-x))


def _silu(x):
    return x * _sigmoid(x)


def _rms_rows(x, g):
    return x * lax.rsqrt(jnp.mean(x * x, axis=-1, keepdims=True) + EPS) * g


def _inproj_kernel(x_ref, xs_ref, g_ref, w_ref, z_ref, zs_ref, xn_ref, xsn_ref):
    i, j = pl.program_id(0), pl.program_id(1)

    @pl.when(j == 0)
    def _():
        xn_ref[...] = _rms_rows(x_ref[...], g_ref[...]).astype(BF16)

    z_ref[...] = jnp.dot(xn_ref[...], w_ref[...], preferred_element_type=F32)

    @pl.when(i == 0)
    def _():
        @pl.when(j == 0)
        def _():
            xsn_ref[...] = _rms_rows(xs_ref[...], g_ref[...]).astype(BF16)

        zs_ref[...] = jnp.dot(xsn_ref[...], w_ref[...], preferred_element_type=F32)


def _inproj(x, x_small, g, w_bf16, tm, tn):
    m, ms = x.shape[0], x_small.shape[0]
    n = w_bf16.shape[1]
    nj = n // tn
    blocks = (tm + ms) * D_MODEL * 4 + D_MODEL * tn * 2 + (tm + ms) * tn * 4
    resident = (tm + ms) * D_MODEL * 2 + tm * tn * 4
    return pl.pallas_call(
        _inproj_kernel,
        grid=(m // tm, nj),
        in_specs=[pl.BlockSpec((tm, D_MODEL), lambda i, j: (i, 0)),
                  pl.BlockSpec((ms, D_MODEL), lambda i, j: (0, 0)),
                  pl.BlockSpec((1, D_MODEL), lambda i, j: (0, 0)),
                  pl.BlockSpec((D_MODEL, tn), lambda i, j: (0, j))],
        out_specs=[pl.BlockSpec((tm, tn), lambda i, j: (i, j)),
                   pl.BlockSpec((ms, tn), lambda i, j: (0, jnp.where(i == 0, j, nj - 1)))],
        out_shape=[jax.ShapeDtypeStruct((m, n), F32), jax.ShapeDtypeStruct((ms, n), F32)],
        scratch_shapes=[pltpu.VMEM((tm, D_MODEL), BF16), pltpu.VMEM((ms, D_MODEL), BF16)],
        compiler_params=_params(_vmem_limit(blocks, resident), 2),
        name="inproj",
    )(x, x_small, g, w_bf16)


def _lower_bound(lb_logits):
    m = jnp.max(lb_logits, axis=0, keepdims=True)
    e = jnp.exp(lb_logits - m)
    return e[0:1, :] / jnp.sum(e, axis=0, keepdims=True)


def _cumsum_rows(x):
    n = x.shape[0]
    row = lax.broadcasted_iota(jnp.int32, x.shape, 0)
    s = 1
    while s < n:
        x = x + jnp.where(row >= s, pltpu.roll(x, s, 0), 0.0)
        s *= 2
    return x


def _gates(zf, lb):
    fg = lb + (1.0 - lb) * _sigmoid(zf)
    return 1.0 - fg, jnp.log(fg)


def _chunk_masks(c, diag_block):
    row_w = lax.broadcasted_iota(jnp.int32, (c, HEAD), 0)
    row = lax.broadcasted_iota(jnp.int32, (c, c), 0)
    col = lax.broadcasted_iota(jnp.int32, (c, c), 1)
    levels = []
    half = c // 2
    while half >= diag_block:
        blk = 2 * half
        is_right = (row_w & (blk - 1)) >= half
        same_blk = (row ^ col) < blk
        valid = same_blk & ((row & (blk - 1)) >= half) & ((col & (blk - 1)) < half)
        levels.append((half, is_right, valid))
        half //= 2
    diag = ((row ^ col) < diag_block) & (col <= row)
    return levels, diag


def _block_rows(b, blk, offset):
    return jnp.concatenate(
        [jnp.broadcast_to(b[s0 + offset:s0 + offset + 1], (blk, HEAD)) for s0 in range(0, b.shape[0], blk)],
        axis=0)


def _diag_scores_exact(q, k, b):
    c = q.shape[0]
    row8 = lax.broadcasted_iota(jnp.int32, (SUB, HEAD), 0)
    lane = lax.broadcasted_iota(jnp.int32, (SUB, c), 1)
    diag = []
    for i in range(c // SUB):
        bi, qi, ki = (t[SUB * i:SUB * (i + 1)] for t in (b, q, k))
        blk = jnp.zeros((SUB, c), F32)
        for s in range(SUB):
            arg = jnp.where(row8 >= s, bi - bi[s:s + 1], NEG_BIG)
            a = jnp.sum(qi * ki[s:s + 1] * jnp.exp(arg), axis=-1, keepdims=True)
            blk = jnp.where(lane == SUB * i + s, a, blk)
        diag.append(blk)
    return jnp.concatenate(diag, axis=0)


def _score_terms(q, k, b, masks, exact):
    levels, diag_mask = masks
    if exact:
        terms = [(None, _diag_scores_exact(q, k, b))]
    else:
        d = b - _block_rows(b, FAST_BLOCK, FAST_BLOCK // 2 - 1)
        xq = (q * jnp.exp(d)).astype(BF16)
        xk = (k * jnp.exp(-d)).astype(BF16)
        terms = [(diag_mask, lax.dot_general(xq, xk, _NT, preferred_element_type=F32))]
    for half, is_right, valid in levels:
        mid = _block_rows(b, 2 * half, half - 1)
        x = (jnp.where(is_right, q, k) * jnp.exp(-jnp.abs(b - mid))).astype(BF16)
        terms.append((valid, lax.dot_general(x, x, _NT, preferred_element_type=F32)))
    return terms


def _sum_terms(terms):
    sc = None
    for mask, p in terms:
        p = p if mask is None else jnp.where(mask, p, 0.0)
        sc = p if sc is None else sc + p
    return sc


def _chunk_state(k, v, b, st):
    b_last = b[b.shape[0] - 1:, :]
    kd = (k * jnp.exp(b_last - b)).astype(BF16)
    return st * jnp.exp(b_last) + lax.dot_general(v.astype(BF16), kd, _TN, preferred_element_type=F32)


def _hgrn_out(o, g, zog):
    return _rms_rows(o, g) * _silu(zog)


def _hgrn_prompt_kernel(zq_ref, zf_ref, zi_ref, zog_ref, mf_ref, mi_ref, lbl_ref, g_ref, wo_ref,
                        o_ref, s_ref, wo_out):
    wo_out[...] = wo_ref[...].astype(BF16)

    n_chunks = zq_ref.shape[0] // CHUNK
    heads = [slice(h * HEAD, (h + 1) * HEAD) for h in range(HEADS_PER_STEP)]
    lbs = [_lower_bound(lbl_ref[:, hs]) for hs in heads]

    def meta_state(hs, lb):
        mk, mlf = _gates(mf_ref[:, hs], lb)
        return _chunk_state(mk, mi_ref[:, hs], _cumsum_rows(mlf), jnp.zeros((HEAD, HEAD), F32))

    def scan(exact, sts):
        masks = _chunk_masks(CHUNK, SUB if exact else FAST_BLOCK)

        def body(c, sts):
            rows = pl.ds(pl.multiple_of(c * CHUNK, CHUNK), CHUNK)
            issued = []
            for hs, lb, st in zip(heads, lbs, sts):
                q = _silu(zq_ref[rows, hs])
                k, lf = _gates(zf_ref[rows, hs], lb)
                v = zi_ref[rows, hs].astype(BF16)
                b = _cumsum_rows(lf)
                o_inter = lax.dot_general((q * jnp.exp(b)).astype(BF16), st.astype(BF16), _NT,
                                          preferred_element_type=F32)
                issued.append((v, o_inter, _score_terms(q, k, b, masks, exact), _chunk_state(k, v, b, st)))
            outs = [o_inter + jnp.dot(_sum_terms(terms).astype(BF16), v, preferred_element_type=F32)
                    for v, o_inter, terms, _ in issued]
            for hs, o in zip(heads, outs):
                o_ref[rows, hs] = _hgrn_out(o, g_ref[:, hs], zog_ref[rows, hs]).astype(o_ref.dtype)
            return tuple(new_st for _, _, _, new_st in issued)

        return lax.fori_loop(0, n_chunks, body, sts, unroll=CHUNK_UNROLL)

    min_log_lb = jnp.min(jnp.log(jnp.concatenate(lbs, axis=1)))
    sts = lax.cond(min_log_lb * (FAST_BLOCK // 2) >= -SAFE_EXP,
                   lambda sts: scan(False, sts), lambda sts: scan(True, sts),
                   tuple(meta_state(hs, lb) for hs, lb in zip(heads, lbs)))
    for h, st in enumerate(sts):
        s_ref[0, h] = st.T


def _hgrn_prompt(z, z_small, lb_logits, hgrn_g, w_out, batch, seq):
    width = HEADS_PER_STEP * HEAD
    hb = A_WIDTH // width
    steps = batch * hb
    assert w_out.shape[0] % (BF16_ROWS * steps) == 0
    wo_rows = w_out.shape[0] // steps
    wo_slab = pl.BlockSpec((wo_rows, D_MODEL), lambda b, h: (b * hb + h, 0))
    col = lambda base: (lambda b, h: (b, base * hb + h))
    meta_blk = z_small.shape[0] // N_META - 1
    meta = lambda base: (lambda b, h: (meta_blk, base * hb + h))
    blocks = (4 * seq * width * 4 + seq * width * 2 + 2 * N_META * width * 4 + width * HEAD * 4
              + wo_rows * D_MODEL * (4 + 2))
    return pl.pallas_call(
        _hgrn_prompt_kernel,
        grid=(batch, hb),
        in_specs=[pl.BlockSpec((seq, width), col(0)),
                  pl.BlockSpec((seq, width), col(1)),
                  pl.BlockSpec((seq, width), col(2)),
                  pl.BlockSpec((seq, width), col(3)),
                  pl.BlockSpec((N_META, width), meta(1)),
                  pl.BlockSpec((N_META, width), meta(2)),
                  pl.BlockSpec((lb_logits.shape[0], width), lambda b, h: (0, h)),
                  pl.BlockSpec((1, width), lambda b, h: (0, h)),
                  wo_slab],
        out_specs=[pl.BlockSpec((seq, width), lambda b, h: (b, h)),
                   pl.BlockSpec((1, HEADS_PER_STEP, HEAD, HEAD), lambda b, h: (b, h, 0, 0)),
                   wo_slab],
        out_shape=[jax.ShapeDtypeStruct((batch * seq, A_WIDTH), BF16),
                   jax.ShapeDtypeStruct((batch, A_HEADS, HEAD, HEAD), F32),
                   jax.ShapeDtypeStruct(w_out.shape, BF16)],
        compiler_params=_params(_vmem_limit(blocks, 4 * 1024 * 1024), 2),
        name="hgrn_prompt",
    )(z, z, z, z, z_small, z_small, lb_logits, hgrn_g, w_out)


SAMPLE_ROWS = 8
SAMPLE_UNROLL = 4
SPLAT_ROWS = 16


def _hgrn_sample_kernel(z_ref, s_ref, lbl_ref, g_ref, o_ref, so_ref):
    row = lax.broadcasted_iota(jnp.int32, (SPLAT_ROWS, HEAD), 0)
    ones_f = jnp.where(row < 3, 1.0, 0.0)
    ones_q = jnp.where(row == 4, 1.0, 0.0)

    def per_seq(r, carry):
        zrow = z_ref[r]
        splats = []
        for h in range(A_HEADS):
            sl = lambda base: zrow[:, base * A_WIDTH + h * HEAD: base * A_WIDTH + (h + 1) * HEAD]
            lb = _lower_bound(lbl_ref[:, h * HEAD:(h + 1) * HEAD])
            q = _silu(sl(0))
            fg = lb + (1.0 - lb) * _sigmoid(sl(1))
            k = 1.0 - fg
            v = sl(2)
            f_hi = fg.astype(BF16).astype(F32)
            f_mid = (fg - f_hi).astype(BF16).astype(F32)
            f_lo = fg - f_hi - f_mid
            a = jnp.where(row == 0, f_hi, jnp.where(row == 1, f_mid, jnp.where(
                row == 2, f_lo, jnp.where(row == 3, k, jnp.where(row == 4, q, 0.0)))))
            bmat = jnp.concatenate([ones_f, ones_q, jnp.where(row == 3, v, 0.0)], axis=1)
            splats.append(lax.dot_general(a.astype(BF16), bmat.astype(BF16), _TN,
                                          preferred_element_type=F32))
        for h, gm in enumerate(splats):
            s_new = gm[:, 0:HEAD] * s_ref[r, h] + gm[:, 2 * HEAD:3 * HEAD]
            so_ref[r, h] = s_new
            o = jnp.sum(gm[:, HEAD:2 * HEAD] * s_new, axis=0, keepdims=True)
            og = zrow[:, 3 * A_WIDTH + h * HEAD: 3 * A_WIDTH + (h + 1) * HEAD]
            o_ref[r, :, h * HEAD:(h + 1) * HEAD] = _hgrn_out(
                o, g_ref[:, h * HEAD:(h + 1) * HEAD], og)
        return carry

    lax.fori_loop(0, SAMPLE_ROWS, per_seq, 0, unroll=SAMPLE_UNROLL)


def _hgrn_sample(z_small, state, lb_logits, hgrn_g):
    nb = state.shape[0]
    blocks = SAMPLE_ROWS * (IN_COLS * 4 + A_WIDTH * 4 + 2 * A_HEADS * HEAD * HEAD * 4)
    return pl.pallas_call(
        _hgrn_sample_kernel,
        grid=(nb // SAMPLE_ROWS,),
        in_specs=[pl.BlockSpec((SAMPLE_ROWS, 1, IN_COLS), lambda i: (i, 0, 0)),
                  pl.BlockSpec((SAMPLE_ROWS, A_HEADS, HEAD, HEAD), lambda i: (i, 0, 0, 0)),
                  pl.BlockSpec(lb_logits.shape, lambda i: (0, 0)),
                  pl.BlockSpec((1, A_WIDTH), lambda i: (0, 0))],
        out_specs=[pl.BlockSpec((SAMPLE_ROWS, 1, A_WIDTH), lambda i: (i, 0, 0)),
                   pl.BlockSpec((SAMPLE_ROWS, A_HEADS, HEAD, HEAD), lambda i: (i, 0, 0, 0))],
        out_shape=[jax.ShapeDtypeStruct((nb, 1, A_WIDTH), F32),
                   jax.ShapeDtypeStruct(state.shape, F32)],
        compiler_params=_params(_vmem_limit(blocks, 2 * 1024 * 1024), 1),
        name="hgrn_sample",
    )(z_small, state, lb_logits, hgrn_g)


def _group_norm_swish(c, gn_g, gn_b):
    mu = jnp.mean(c, axis=-1, keepdims=True)
    d = c - mu
    var = jnp.mean(d * d, axis=-1, keepdims=True)
    return _silu(d * lax.rsqrt(var + EPS) * gn_g + gn_b)


CONV_PAD = 32
CONV_TILE = 256


def _conv_prompt_kernel(ga_ref, gb_ref, ma_ref, mb_ref, w_ref, cb_ref, gg_ref, gbias_ref,
                        wg_ref, wu_ref, wd_ref, o_ref, buf_ref, wg_out, wu_out, wd_out, u_ref):
    wg_out[...] = wg_ref[...].astype(BF16)
    wu_out[...] = wu_ref[...].astype(BF16)
    wd_out[...] = wd_ref[...].astype(BF16)

    seq = ga_ref.shape[0]
    u_ref[0:CONV_PAD - N_META, :] = jnp.zeros((CONV_PAD - N_META, GROUP), F32)
    u_ref[CONV_PAD - N_META:CONV_PAD, :] = ma_ref[...] * _sigmoid(mb_ref[...])
    u_ref[CONV_PAD:, :] = ga_ref[...] * _sigmoid(gb_ref[...])
    bias = cb_ref[...]
    for t0 in range(0, seq, CONV_TILE):
        acc = jnp.zeros((CONV_TILE, GROUP), F32) + bias
        for j in range(CONV_W):
            start = t0 + CONV_PAD - HIST + j
            acc = acc + w_ref[j:j + 1, :] * u_ref[start:start + CONV_TILE, :]
        o_ref[t0:t0 + CONV_TILE, :] = _group_norm_swish(acc, gg_ref[...], gbias_ref[...]).astype(
            o_ref.dtype)
    buf_ref[0] = u_ref[CONV_PAD + seq - HIST:CONV_PAD + seq, :]


def _conv_prompt(z, z_small, conv_w, conv_b, gn_g, gn_b, w_gate, w_up, w_down, batch, seq):
    ga0 = 4 * A_WIDTH // GROUP
    gb0 = ga0 + B_WIDTH // GROUP
    meta_blk = z_small.shape[0] // N_META - 1
    vec = pl.BlockSpec((1, GROUP), lambda b, g: (0, g))
    steps = batch * B_GROUPS
    assert D_MODEL % (BF16_ROWS * steps) == 0 and D_FF % (BF16_ROWS * steps) == 0
    up_rows, down_rows = D_MODEL // steps, D_FF // steps
    slab = lambda rows, cols: pl.BlockSpec((rows, cols), lambda b, g: (b * B_GROUPS + g, 0))
    blocks = (2 * seq * GROUP * 4 + seq * GROUP * 2 + 32 * GROUP * 4 * 4
              + (2 * up_rows * D_FF + down_rows * D_MODEL) * (4 + 2))
    return pl.pallas_call(
        _conv_prompt_kernel,
        grid=(batch, B_GROUPS),
        in_specs=[pl.BlockSpec((seq, GROUP), lambda b, g: (b, ga0 + g)),
                  pl.BlockSpec((seq, GROUP), lambda b, g: (b, gb0 + g)),
                  pl.BlockSpec((N_META, GROUP), lambda b, g: (meta_blk, ga0 + g)),
                  pl.BlockSpec((N_META, GROUP), lambda b, g: (meta_blk, gb0 + g)),
                  pl.BlockSpec((CONV_W, GROUP), lambda b, g: (0, g)),
                  vec, vec, vec,
                  slab(up_rows, D_FF), slab(up_rows, D_FF), slab(down_rows, D_MODEL)],
        out_specs=[pl.BlockSpec((seq, GROUP), lambda b, g: (b, g)),
                   pl.BlockSpec((1, HIST, GROUP), lambda b, g: (b, 0, g)),
                   slab(up_rows, D_FF), slab(up_rows, D_FF), slab(down_rows, D_MODEL)],
        out_shape=[jax.ShapeDtypeStruct((batch * seq, B_WIDTH), BF16),
                   jax.ShapeDtypeStruct((batch, HIST, B_WIDTH), F32),
                   jax.ShapeDtypeStruct((D_MODEL, D_FF), BF16),
                   jax.ShapeDtypeStruct((D_MODEL, D_FF), BF16),
                   jax.ShapeDtypeStruct((D_FF, D_MODEL), BF16)],
        scratch_shapes=[pltpu.VMEM((CONV_PAD + seq, GROUP), F32)],
        compiler_params=_params(_vmem_limit(blocks, (CONV_PAD + seq) * GROUP * 4 + 4 * 1024 * 1024), 2),
        name="conv_prompt",
    )(z, z, z_small, z_small, conv_w, conv_b, gn_g, gn_b, w_gate, w_up, w_down)


CONV_SEQS = 16


def _conv_sample_kernel(ga_ref, gb_ref, st_ref, w_ref, cb_ref, gg_ref, gbias_ref, o_ref, buf_ref):
    u = ga_ref[...] * _sigmoid(gb_ref[...])
    c = w_ref[HIST:CONV_W, :] * u + cb_ref[...]
    for j in range(HIST):
        c = c + w_ref[j:j + 1, :] * st_ref[j]
    for g in range(B_GROUPS):
        sl = slice(g * GROUP, (g + 1) * GROUP)
        o_ref[:, sl] = _group_norm_swish(c[:, sl], gg_ref[:, sl], gbias_ref[:, sl])
    for j in range(HIST - 1):
        buf_ref[j] = st_ref[j + 1]
    buf_ref[HIST - 1] = u


def _conv_sample(z_small, state_t, conv_w, conv_b, gn_g, gn_b):
    nb = state_t.shape[1]
    ga0 = 4 * A_WIDTH // B_WIDTH
    vec = pl.BlockSpec((1, B_WIDTH), lambda i: (0, 0))
    blocks = CONV_SEQS * B_WIDTH * 4 * (3 + 2 * HIST) + 32 * B_WIDTH * 4
    return pl.pallas_call(
        _conv_sample_kernel,
        grid=(nb // CONV_SEQS,),
        in_specs=[pl.BlockSpec((CONV_SEQS, B_WIDTH), lambda i: (i, ga0)),
                  pl.BlockSpec((CONV_SEQS, B_WIDTH), lambda i: (i, ga0 + 1)),
                  pl.BlockSpec((HIST, CONV_SEQS, B_WIDTH), lambda i: (0, i, 0)),
                  pl.BlockSpec((CONV_W, B_WIDTH), lambda i: (0, 0)),
                  vec, vec, vec],
        out_specs=[pl.BlockSpec((CONV_SEQS, B_WIDTH), lambda i: (i, 0)),
                   pl.BlockSpec((HIST, CONV_SEQS, B_WIDTH), lambda i: (0, i, 0))],
        out_shape=[jax.ShapeDtypeStruct((nb, B_WIDTH), F32),
                   jax.ShapeDtypeStruct(state_t.shape, F32)],
        compiler_params=_params(_vmem_limit(blocks, 2 * 1024 * 1024), 1),
        name="conv_sample",
    )(z_small, z_small, state_t, conv_w, conv_b, gn_g, gn_b)


def _outproj_kernel(oa_ref, ob_ref, x_ref, wa_ref, wb_ref, g_ref, h_ref, hf_ref):
    h = x_ref[...]
    h = h + jnp.dot(oa_ref[...].astype(BF16), wa_ref[...], preferred_element_type=F32)
    h = h + jnp.dot(ob_ref[...].astype(BF16), wb_ref[...], preferred_element_type=F32)
    h_ref[...] = h
    hf_ref[...] = _rms_rows(h, g_ref[...]).astype(BF16)


def _outproj(oa, ob, x, w_bf16, g, tm):
    m = x.shape[0]
    act = oa.dtype.itemsize
    blocks = 2 * tm * A_WIDTH * act + tm * D_MODEL * (4 + 4 + 2) + D_MODEL * D_MODEL * 2
    return pl.pallas_call(
        _outproj_kernel,
        grid=(m // tm,),
        in_specs=[pl.BlockSpec((tm, A_WIDTH), lambda i: (i, 0)),
                  pl.BlockSpec((tm, B_WIDTH), lambda i: (i, 0)),
                  pl.BlockSpec((tm, D_MODEL), lambda i: (i, 0)),
                  pl.BlockSpec((A_WIDTH, D_MODEL), lambda i: (0, 0)),
                  pl.BlockSpec((B_WIDTH, D_MODEL), lambda i: (1, 0)),
                  pl.BlockSpec((1, D_MODEL), lambda i: (0, 0))],
        out_specs=[pl.BlockSpec((tm, D_MODEL), lambda i: (i, 0)),
                   pl.BlockSpec((tm, D_MODEL), lambda i: (i, 0))],
        out_shape=[jax.ShapeDtypeStruct((m, D_MODEL), F32),
                   jax.ShapeDtypeStruct((m, D_MODEL), BF16)],
        compiler_params=_params(_vmem_limit(blocks, 2 * tm * D_MODEL * 4), 1),
        name="outproj",
    )(oa, ob, x, w_bf16, w_bf16, g)


RESID_ROWS = 128


def _swiglu_down(hf, wg_ref, wu_ref, wd_ref):
    gate = jnp.dot(hf, wg_ref[...], preferred_element_type=F32)
    up = jnp.dot(hf, wu_ref[...], preferred_element_type=F32)
    return jnp.dot((_silu(gate) * up).astype(BF16), wd_ref[...], preferred_element_type=F32)


def _ffn_kernel(hf_ref, h_ref, hfs_ref, hs_ref, wg_ref, wu_ref, wd_ref, g_ref, y_ref, ys_ref):
    f = pl.program_id(1)
    last = pl.num_programs(1) - 1
    resid = h_ref.shape[0]

    @pl.when(f == 0)
    def _():
        y_ref[...] = jnp.zeros(y_ref.shape, F32)

    y_ref[...] += _swiglu_down(hf_ref[...], wg_ref, wu_ref, wd_ref)

    @pl.when(f < y_ref.shape[0] // resid)
    def _():
        rows = pl.ds(pl.multiple_of(f * resid, resid), resid)
        y_ref[rows, :] += h_ref[...]

    @pl.when(f == last)
    def _():
        y_ref[...] = _rms_rows(y_ref[...], g_ref[...])

    @pl.when(pl.program_id(0) == 0)
    def _():
        @pl.when(f == 0)
        def _():
            ys_ref[...] = hs_ref[...]

        ys_ref[...] += _swiglu_down(hfs_ref[...], wg_ref, wu_ref, wd_ref)

        @pl.when(f == last)
        def _():
            ys_ref[...] = _rms_rows(ys_ref[...], g_ref[...])


def _ffn(hf, h1, hf_s, h1_s, wg, wu, wd, g, tm, tf):
    m, ms = hf.shape[0], hf_s.shape[0]
    resid = min(RESID_ROWS, tm)
    slabs = tm // resid
    assert tm % resid == 0 and slabs <= D_FF // tf
    blocks = (tm * D_MODEL * (2 + 4) + resid * D_MODEL * 4 + 3 * D_MODEL * tf * 2
              + ms * D_MODEL * (2 + 4 + 4))
    resident = tm * D_MODEL * 4 + 3 * tm * tf * 4
    whole = lambda rows: pl.BlockSpec((rows, D_MODEL), lambda i, f: (0, 0))
    return pl.pallas_call(
        _ffn_kernel,
        grid=(m // tm, D_FF // tf),
        in_specs=[pl.BlockSpec((tm, D_MODEL), lambda i, f: (i, 0)),
                  pl.BlockSpec((resid, D_MODEL), lambda i, f: (i * slabs + jnp.minimum(f, slabs - 1), 0)),
                  whole(ms), whole(ms),
                  pl.BlockSpec((D_MODEL, tf), lambda i, f: (0, f)),
                  pl.BlockSpec((D_MODEL, tf), lambda i, f: (0, f)),
                  pl.BlockSpec((tf, D_MODEL), lambda i, f: (f, 0)),
                  pl.BlockSpec((1, D_MODEL), lambda i, f: (0, 0))],
        out_specs=[pl.BlockSpec((tm, D_MODEL), lambda i, f: (i, 0)), whole(ms)],
        out_shape=[jax.ShapeDtypeStruct((m, D_MODEL), F32),
                   jax.ShapeDtypeStruct((ms, D_MODEL), F32)],
        compiler_params=_params(_vmem_limit(blocks, resident), 2),
        name="ffn",
    )(hf, h1, hf_s, h1_s, wg, wu, wd, g)


def kernel(x_prompt, x_sample, state_hgrn, state_conv, meta_tokens, norm_mix_g, w_in, lb_logits,
           hgrn_norm_g, conv_w, conv_b, gn_g, gn_b, w_out, norm_ffn_g, w_ffn_gate, w_ffn_up,
           w_ffn_down, norm_final_g):
    batch, seq, _ = x_prompt.shape
    nb = x_sample.shape[0]
    assert x_sample.shape[1] == 1 and norm_mix_g.shape[0] == 1
    assert seq % (CHUNK * CHUNK_UNROLL) == 0 and seq % CONV_TILE == 0
    assert nb % N_META == 0 and nb % CONV_SEQS == 0 and nb % SAMPLE_ROWS == 0

    w_in_b = w_in[0].astype(BF16)
    g_final = norm_final_g[None, :]

    xp = x_prompt.reshape(batch * seq, D_MODEL)
    xs = x_sample.reshape(nb, D_MODEL)
    x_small = jnp.concatenate([xs, meta_tokens], axis=0)

    z, z_small = _inproj(xp, x_small, norm_mix_g, w_in_b, tm=1024, tn=1536)

    oa_p, s_p, w_out_b = _hgrn_prompt(z, z_small, lb_logits, hgrn_norm_g, w_out[0], batch, seq)
    ob_p, c_p, wg_b, wu_b, wd_b = _conv_prompt(z, z_small, conv_w[0], conv_b, gn_g, gn_b,
                                               w_ffn_gate[0], w_ffn_up[0], w_ffn_down[0], batch, seq)
    z_s = z_small[:nb].reshape(nb, 1, IN_COLS)
    oa_s, s_s = _hgrn_sample(z_s, state_hgrn[0], lb_logits, hgrn_norm_g)
    ob_s, c_s = _conv_sample(z_small, jnp.transpose(state_conv[0], (1, 0, 2)), conv_w[0], conv_b, gn_g, gn_b)
    c_s = jnp.transpose(c_s, (1, 0, 2))
    oa_s = oa_s.reshape(nb, A_WIDTH)

    h_p, hf_p = _outproj(oa_p, ob_p, xp, w_out_b, norm_ffn_g, tm=512)
    h_s, hf_s = _outproj(oa_s, ob_s, xs, w_out_b, norm_ffn_g, tm=nb)

    y_p, y_s = _ffn(hf_p, h_p, hf_s, h_s, wg_b, wu_b, wd_b, g_final, tm=1024, tf=512)

    return (y_p.reshape(batch, seq, D_MODEL), y_s.reshape(nb, 1, D_MODEL),
            s_p[None], c_p[None], s_s[None], c_s[None])
```

```python
import jax
import jax.numpy as jnp
from jax import lax
from jax.experimental import pallas as pl
from jax.experimental.pallas import tpu as pltpu

F32 = jnp.float32
BF16 = jnp.bfloat16

D_MODEL = 2048
N_META = 16
A_WIDTH = 1024
B_WIDTH = 1024
HEAD = 128
A_HEADS = A_WIDTH // HEAD
CONV_W = 31
HIST = CONV_W - 1
B_GROUPS = 8
GROUP = B_WIDTH // B_GROUPS
D_FF = 5632
IN_COLS = 4 * A_WIDTH + 2 * B_WIDTH
EPS = 1e-6

V7X_LANES = 128
V7X_SUBLANES = 8
V7X_VMEM_BYTES = 64 * 1024 * 1024
V7X_VMEM_USABLE = V7X_VMEM_BYTES - 8 * 1024 * 1024
BF16_ROWS = 2 * V7X_SUBLANES

CHUNK = 64
SUB = V7X_SUBLANES
NEG_BIG = -1e30
FAST_BLOCK = 32
SAFE_EXP = 60.0
HEADS_PER_STEP = 4
CHUNK_UNROLL = 4

_NT = (((1,), (1,)), ((), ()))
_TN = (((0,), (0,)), ((), ()))


def _vmem_limit(pipelined_block_bytes, resident_bytes):
    est = 2 * pipelined_block_bytes + resident_bytes
    return int(min(max(est, 16 * 1024 * 1024), V7X_VMEM_USABLE))


def _params(vmem_bytes, ndims):
    return pltpu.CompilerParams(dimension_semantics=("arbitrary",) * ndims,
                                vmem_limit_bytes=vmem_bytes)


def _sigmoid(x):
    return 1.0 / (1.0 + jnp.exp(-x))


def _silu(x):
    return x * _sigmoid(x)


def _rms_rows(x, g):
    return x * lax.rsqrt(jnp.mean(x * x, axis=-1, keepdims=True) + EPS) * g


def _inproj_kernel(x_ref, xs_ref, g_ref, w_ref, z_ref, zs_ref, xn_ref, xsn_ref):
    i, j = pl.program_id(0), pl.program_id(1)

    @pl.when(j == 0)
    def _():
        xn_ref[...] = _rms_rows(x_ref[...], g_ref[...]).astype(BF16)

    z_ref[...] = jnp.dot(xn_ref[...], w_ref[...], preferred_element_type=F32)

    @pl.when(i == 0)
    def _():
        @pl.when(j == 0)
        def _():
            xsn_ref[...] = _rms_rows(xs_ref[...], g_ref[...]).astype(BF16)

        zs_ref[...] = jnp.dot(xsn_ref[...], w_ref[...], preferred_element_type=F32)


def _inproj(x, x_small, g, w_bf16, tm, tn):
    m, ms = x.shape[0], x_small.shape[0]
    n = w_bf16.shape[1]
    nj = n // tn
    blocks = (tm + ms) * D_MODEL * 4 + D_MODEL * tn * 2 + (tm + ms) * tn * 4
    resident = (tm + ms) * D_MODEL * 2 + tm * tn * 4
    return pl.pallas_call(
        _inproj_kernel,
        grid=(m // tm, nj),
        in_specs=[pl.BlockSpec((tm, D_MODEL), lambda i, j: (i, 0)),
                  pl.BlockSpec((ms, D_MODEL), lambda i, j: (0, 0)),
                  pl.BlockSpec((1, D_MODEL), lambda i, j: (0, 0)),
                  pl.BlockSpec((D_MODEL, tn), lambda i, j: (0, j))],
        out_specs=[pl.BlockSpec((tm, tn), lambda i, j: (i, j)),
                   pl.BlockSpec((ms, tn), lambda i, j: (0, jnp.where(i == 0, j, nj - 1)))],
        out_shape=[jax.ShapeDtypeStruct((m, n), F32), jax.ShapeDtypeStruct((ms, n), F32)],
        scratch_shapes=[pltpu.VMEM((tm, D_MODEL), BF16), pltpu.VMEM((ms, D_MODEL), BF16)],
        compiler_params=_params(_vmem_limit(blocks, resident), 2),
        name="inproj",
    )(x, x_small, g, w_bf16)


def _lower_bound(lb_logits):
    m = jnp.max(lb_logits, axis=0, keepdims=True)
    e = jnp.exp(lb_logits - m)
    return e[0:1, :] / jnp.sum(e, axis=0, keepdims=True)


def _cumsum_rows(x):
    n = x.shape[0]
    row = lax.broadcasted_iota(jnp.int32, x.shape, 0)
    s = 1
    while s < n:
        x = x + jnp.where(row >= s, pltpu.roll(x, s, 0), 0.0)
        s *= 2
    return x


def _gates(zf, lb):
    fg = lb + (1.0 - lb) * _sigmoid(zf)
    return 1.0 - fg, jnp.log(fg)


def _chunk_masks(c, diag_block):
    row_w = lax.broadcasted_iota(jnp.int32, (c, HEAD), 0)
    row = lax.broadcasted_iota(jnp.int32, (c, c), 0)
    col = lax.broadcasted_iota(jnp.int32, (c, c), 1)
    levels = []
    half = c // 2
    while half >= diag_block:
        blk = 2 * half
        is_right = (row_w & (blk - 1)) >= half
        same_blk = (row ^ col) < blk
        valid = same_blk & ((row & (blk - 1)) >= half) & ((col & (blk - 1)) < half)
        levels.append((half, is_right, valid))
        half //= 2
    diag = ((row ^ col) < diag_block) & (col <= row)
    return levels, diag


def _block_rows(b, blk, offset):
    return jnp.concatenate(
        [jnp.broadcast_to(b[s0 + offset:s0 + offset + 1], (blk, HEAD)) for s0 in range(0, b.shape[0], blk)],
        axis=0)


def _diag_scores_exact(q, k, b):
    c = q.shape[0]
    row8 = lax.broadcasted_iota(jnp.int32, (SUB, HEAD), 0)
    lane = lax.broadcasted_iota(jnp.int32, (SUB, c), 1)
    diag = []
    for i in range(c // SUB):
        bi, qi, ki = (t[SUB * i:SUB * (i + 1)] for t in (b, q, k))
        blk = jnp.zeros((SUB, c), F32)
        for s in range(SUB):
            arg = jnp.where(row8 >= s, bi - bi[s:s + 1], NEG_BIG)
            a = jnp.sum(qi * ki[s:s + 1] * jnp.exp(arg), axis=-1, keepdims=True)
            blk = jnp.where(lane == SUB * i + s, a, blk)
        diag.append(blk)
    return jnp.concatenate(diag, axis=0)


def _score_terms(q, k, b, masks, exact):
    levels, diag_mask = masks
    if exact:
        terms = [(None, _diag_scores_exact(q, k, b))]
    else:
        d = b - _block_rows(b, FAST_BLOCK, FAST_BLOCK // 2 - 1)
        xq = (q * jnp.exp(d)).astype(BF16)
        xk = (k * jnp.exp(-d)).astype(BF16)
        terms = [(diag_mask, lax.dot_general(xq, xk, _NT, preferred_element_type=F32))]
    for half, is_right, valid in levels:
        mid = _block_rows(b, 2 * half, half - 1)
        x = (jnp.where(is_right, q, k) * jnp.exp(-jnp.abs(b - mid))).astype(BF16)
        terms.append((valid, lax.dot_general(x, x, _NT, preferred_element_type=F32)))
    return terms


def _sum_terms(terms):
    sc = None
    for mask, p in terms:
        p = p if mask is None else jnp.where(mask, p, 0.0)
        sc = p if sc is None else sc + p
    return sc


def _chunk_state(k, v, b, st):
    b_last = b[b.shape[0] - 1:, :]
    kd = (k * jnp.exp(b_last - b)).astype(BF16)
    return st * jnp.exp(b_last) + lax.dot_general(v.astype(BF16), kd, _TN, preferred_element_type=F32)


def _hgrn_out(o, g, zog):
    return _rms_rows(o, g) * _silu(zog)


def _hgrn_prompt_kernel(zq_ref, zf_ref, zi_ref, zog_ref, mf_ref, mi_ref, lbl_ref, g_ref, wo_ref,
                        o_ref, s_ref, wo_out):
    wo_out[...] = wo_ref[...].astype(BF16)

    n_chunks = zq_ref.shape[0] // CHUNK
    heads = [slice(h * HEAD, (h + 1) * HEAD) for h in range(HEADS_PER_STEP)]
    lbs = [_lower_bound(lbl_ref[:, hs]) for hs in heads]

    def meta_state(hs, lb):
        mk, mlf = _gates(mf_ref[:, hs], lb)
        return _chunk_state(mk, mi_ref[:, hs], _cumsum_rows(mlf), jnp.zeros((HEAD, HEAD), F32))

    def scan(exact, sts):
        masks = _chunk_masks(CHUNK, SUB if exact else FAST_BLOCK)

        def body(c, sts):
            rows = pl.ds(pl.multiple_of(c * CHUNK, CHUNK), CHUNK)
            issued = []
            for hs, lb, st in zip(heads, lbs, sts):
                q = _silu(zq_ref[rows, hs])
                k, lf = _gates(zf_ref[rows, hs], lb)
                v = zi_ref[rows, hs].astype(BF16)
                b = _cumsum_rows(lf)
                o_inter = lax.dot_general((q * jnp.exp(b)).astype(BF16), st.astype(BF16), _NT,
                                          preferred_element_type=F32)
                issued.append((v, o_inter, _score_terms(q, k, b, masks, exact), _chunk_state(k, v, b, st)))
            outs = [o_inter + jnp.dot(_sum_terms(terms).astype(BF16), v, preferred_element_type=F32)
                    for v, o_inter, terms, _ in issued]
            for hs, o in zip(heads, outs):
                o_ref[rows, hs] = _hgrn_out(o, g_ref[:, hs], zog_ref[rows, hs]).astype(o_ref.dtype)
            return tuple(new_st for _, _, _, new_st in issued)

        return lax.fori_loop(0, n_chunks, body, sts, unroll=CHUNK_UNROLL)

    min_log_lb = jnp.min(jnp.log(jnp.concatenate(lbs, axis=1)))
    sts = lax.cond(min_log_lb * (FAST_BLOCK // 2) >= -SAFE_EXP,
                   lambda sts: scan(False, sts), lambda sts: scan(True, sts),
                   tuple(meta_state(hs, lb) for hs, lb in zip(heads, lbs)))
    for h, st in enumerate(sts):
        s_ref[0, h] = st.T


def _hgrn_prompt(z, z_small, lb_logits, hgrn_g, w_out, batch, seq):
    width = HEADS_PER_STEP * HEAD
    hb = A_WIDTH // width
    steps = batch * hb
    assert w_out.shape[0] % (BF16_ROWS * steps) == 0
    wo_rows = w_out.shape[0] // steps
    wo_slab = pl.BlockSpec((wo_rows, D_MODEL), lambda b, h: (b * hb + h, 0))
    col = lambda base: (lambda b, h: (b, base * hb + h))
    meta_blk = z_small.shape[0] // N_META - 1
    meta = lambda base: (lambda b, h: (meta_blk, base * hb + h))
    blocks = (4 * seq * width * 4 + seq * width * 2 + 2 * N_META * width * 4 + width * HEAD * 4
              + wo_rows * D_MODEL * (4 + 2))
    return pl.pallas_call(
        _hgrn_prompt_kernel,
        grid=(batch, hb),
        in_specs=[pl.BlockSpec((seq, width), col(0)),
                  pl.BlockSpec((seq, width), col(1)),
                  pl.BlockSpec((seq, width), col(2)),
                  pl.BlockSpec((seq, width), col(3)),
                  pl.BlockSpec((N_META, width), meta(1)),
                  pl.BlockSpec((N_META, width), meta(2)),
                  pl.BlockSpec((lb_logits.shape[0], width), lambda b, h: (0, h)),
                  pl.BlockSpec((1, width), lambda b, h: (0, h)),
                  wo_slab],
        out_specs=[pl.BlockSpec((seq, width), lambda b, h: (b, h)),
                   pl.BlockSpec((1, HEADS_PER_STEP, HEAD, HEAD), lambda b, h: (b, h, 0, 0)),
                   wo_slab],
        out_shape=[jax.ShapeDtypeStruct((batch * seq, A_WIDTH), BF16),
                   jax.ShapeDtypeStruct((batch, A_HEADS, HEAD, HEAD), F32),
                   jax.ShapeDtypeStruct(w_out.shape, BF16)],
        compiler_params=_params(_vmem_limit(blocks, 4 * 1024 * 1024), 2),
        name="hgrn_prompt",
    )(z, z, z, z, z_small, z_small, lb_logits, hgrn_g, w_out)


SAMPLE_ROWS = 16
SAMPLE_UNROLL = 4
SPLAT_ROWS = BF16_ROWS


def _hgrn_sample_kernel(z_ref, s_ref, lbl_ref, g_ref, o_ref, so_ref):
    row = lax.broadcasted_iota(jnp.int32, (SPLAT_ROWS, HEAD), 0)
    ones_f = jnp.where(row < 3, 1.0, 0.0)
    ones_q = jnp.where(row == 4, 1.0, 0.0)

    def per_seq(r, carry):
        zrow = z_ref[r]
        splats = []
        for h in range(A_HEADS):
            sl = lambda base: zrow[:, base * A_WIDTH + h * HEAD: base * A_WIDTH + (h + 1) * HEAD]
            lb = _lower_bound(lbl_ref[:, h * HEAD:(h + 1) * HEAD])
            q = _silu(sl(0))
            fg = lb + (1.0 - lb) * _sigmoid(sl(1))
            k = 1.0 - fg
            v = sl(2)
            f_hi = fg.astype(BF16).astype(F32)
            f_mid = (fg - f_hi).astype(BF16).astype(F32)
            f_lo = fg - f_hi - f_mid
            a = jnp.where(row == 0, f_hi, jnp.where(row == 1, f_mid, jnp.where(
                row == 2, f_lo, jnp.where(row == 3, k, jnp.where(row == 4, q, 0.0)))))
            bmat = jnp.concatenate([ones_f, ones_q, jnp.where(row == 3, v, 0.0)], axis=1)
            splats.append(lax.dot_general(a.astype(BF16), bmat.astype(BF16), _TN,
                                          preferred_element_type=F32))
        for h, gm in enumerate(splats):
            s_new = gm[:, 0:HEAD] * s_ref[r, h] + gm[:, 2 * HEAD:3 * HEAD]
            so_ref[r, h] = s_new
            o = jnp.sum(gm[:, HEAD:2 * HEAD] * s_new, axis=0, keepdims=True)
            og = zrow[:, 3 * A_WIDTH + h * HEAD: 3 * A_WIDTH + (h + 1) * HEAD]
            o_ref[r, :, h * HEAD:(h + 1) * HEAD] = _hgrn_out(
                o, g_ref[:, h * HEAD:(h + 1) * HEAD], og)
        return carry

    lax.fori_loop(0, SAMPLE_ROWS, per_seq, 0, unroll=SAMPLE_UNROLL)


def _hgrn_sample(z_small, state, lb_logits, hgrn_g):
    nb = state.shape[0]
    blocks = SAMPLE_ROWS * (IN_COLS * 4 + A_WIDTH * 4 + 2 * A_HEADS * HEAD * HEAD * 4)
    return pl.pallas_call(
        _hgrn_sample_kernel,
        grid=(nb // SAMPLE_ROWS,),
        in_specs=[pl.BlockSpec((SAMPLE_ROWS, 1, IN_COLS), lambda i: (i, 0, 0)),
                  pl.BlockSpec((SAMPLE_ROWS, A_HEADS, HEAD, HEAD), lambda i: (i, 0, 0, 0)),
                  pl.BlockSpec(lb_logits.shape, lambda i: (0, 0)),
                  pl.BlockSpec((1, A_WIDTH), lambda i: (0, 0))],
        out_specs=[pl.BlockSpec((SAMPLE_ROWS, 1, A_WIDTH), lambda i: (i, 0, 0)),
                   pl.BlockSpec((SAMPLE_ROWS, A_HEADS, HEAD, HEAD), lambda i: (i, 0, 0, 0))],
        out_shape=[jax.ShapeDtypeStruct((nb, 1, A_WIDTH), F32),
                   jax.ShapeDtypeStruct(state.shape, F32)],
        compiler_params=_params(_vmem_limit(blocks, 2 * 1024 * 1024), 1),
        name="hgrn_sample",
    )(z_small, state, lb_logits, hgrn_g)


def _group_norm_swish(c, gn_g, gn_b):
    mu = jnp.mean(c, axis=-1, keepdims=True)
    d = c - mu
    var = jnp.mean(d * d, axis=-1, keepdims=True)
    return _silu(d * lax.rsqrt(var + EPS) * gn_g + gn_b)


CONV_PAD = 32
CONV_TILE = 256


def _conv_prompt_kernel(ga_ref, gb_ref, ma_ref, mb_ref, w_ref, cb_ref, gg_ref, gbias_ref,
                        wg_ref, wu_ref, wd_ref, o_ref, buf_ref, wg_out, wu_out, wd_out, u_ref):
    wg_out[...] = wg_ref[...].astype(BF16)
    wu_out[...] = wu_ref[...].astype(BF16)
    wd_out[...] = wd_ref[...].astype(BF16)

    seq = ga_ref.shape[0]
    u_ref[0:CONV_PAD - N_META, :] = jnp.zeros((CONV_PAD - N_META, GROUP), F32)
    u_ref[CONV_PAD - N_META:CONV_PAD, :] = ma_ref[...] * _sigmoid(mb_ref[...])
    u_ref[CONV_PAD:, :] = ga_ref[...] * _sigmoid(gb_ref[...])
    bias = cb_ref[...]
    for t0 in range(0, seq, CONV_TILE):
        acc = jnp.zeros((CONV_TILE, GROUP), F32) + bias
        for j in range(CONV_W):
            start = t0 + CONV_PAD - HIST + j
            acc = acc + w_ref[j:j + 1, :] * u_ref[start:start + CONV_TILE, :]
        o_ref[t0:t0 + CONV_TILE, :] = _group_norm_swish(acc, gg_ref[...], gbias_ref[...]).astype(
            o_ref.dtype)
    buf_ref[0] = u_ref[CONV_PAD + seq - HIST:CONV_PAD + seq, :]


def _conv_prompt(z, z_small, conv_w, conv_b, gn_g, gn_b, w_gate, w_up, w_down, batch, seq):
    ga0 = 4 * A_WIDTH // GROUP
    gb0 = ga0 + B_WIDTH // GROUP
    meta_blk = z_small.shape[0] // N_META - 1
    vec = pl.BlockSpec((1, GROUP), lambda b, g: (0, g))
    steps = batch * B_GROUPS
    assert D_MODEL % (BF16_ROWS * steps) == 0 and D_FF % (BF16_ROWS * steps) == 0
    up_rows, down_rows = D_MODEL // steps, D_FF // steps
    slab = lambda rows, cols: pl.BlockSpec((rows, cols), lambda b, g: (b * B_GROUPS + g, 0))
    blocks = (2 * seq * GROUP * 4 + seq * GROUP * 2 + 32 * GROUP * 4 * 4
              + (2 * up_rows * D_FF + down_rows * D_MODEL) * (4 + 2))
    return pl.pallas_call(
        _conv_prompt_kernel,
        grid=(batch, B_GROUPS),
        in_specs=[pl.BlockSpec((seq, GROUP), lambda b, g: (b, ga0 + g)),
                  pl.BlockSpec((seq, GROUP), lambda b, g: (b, gb0 + g)),
                  pl.BlockSpec((N_META, GROUP), lambda b, g: (meta_blk, ga0 + g)),
                  pl.BlockSpec((N_META, GROUP), lambda b, g: (meta_blk, gb0 + g)),
                  pl.BlockSpec((CONV_W, GROUP), lambda b, g: (0, g)),
                  vec, vec, vec,
                  slab(up_rows, D_FF), slab(up_rows, D_FF), slab(down_rows, D_MODEL)],
        out_specs=[pl.BlockSpec((seq, GROUP), lambda b, g: (b, g)),
                   pl.BlockSpec((1, HIST, GROUP), lambda b, g: (b, 0, g)),
                   slab(up_rows, D_FF), slab(up_rows, D_FF), slab(down_rows, D_MODEL)],
        out_shape=[jax.ShapeDtypeStruct((batch * seq, B_WIDTH), BF16),
                   jax.ShapeDtypeStruct((batch, HIST, B_WIDTH), F32),
                   jax.ShapeDtypeStruct((D_MODEL, D_FF), BF16),
                   jax.ShapeDtypeStruct((D_MODEL, D_FF), BF16),
                   jax.ShapeDtypeStruct((D_FF, D_MODEL), BF16)],
        scratch_shapes=[pltpu.VMEM((CONV_PAD + seq, GROUP), F32)],
        compiler_params=_params(_vmem_limit(blocks, (CONV_PAD + seq) * GROUP * 4 + 4 * 1024 * 1024), 2),
        name="conv_prompt",
    )(z, z, z_small, z_small, conv_w, conv_b, gn_g, gn_b, w_gate, w_up, w_down)


CONV_SEQS = 16


def _conv_sample_kernel(ga_ref, gb_ref, st_ref, w_ref, cb_ref, gg_ref, gbias_ref, o_ref, buf_ref):
    u = ga_ref[...] * _sigmoid(gb_ref[...])
    c = w_ref[HIST:CONV_W, :] * u + cb_ref[...]
    for j in range(HIST):
        c = c + w_ref[j:j + 1, :] * st_ref[j]
    for g in range(B_GROUPS):
        sl = slice(g * GROUP, (g + 1) * GROUP)
        o_ref[:, sl] = _group_norm_swish(c[:, sl], gg_ref[:, sl], gbias_ref[:, sl])
    for j in range(HIST - 1):
        buf_ref[j] = st_ref[j + 1]
    buf_ref[HIST - 1] = u


def _conv_sample(z_small, state_t, conv_w, conv_b, gn_g, gn_b):
    nb = state_t.shape[1]
    ga0 = 4 * A_WIDTH // B_WIDTH
    vec = pl.BlockSpec((1, B_WIDTH), lambda i: (0, 0))
    blocks = CONV_SEQS * B_WIDTH * 4 * (3 + 2 * HIST) + 32 * B_WIDTH * 4
    return pl.pallas_call(
        _conv_sample_kernel,
        grid=(nb // CONV_SEQS,),
        in_specs=[pl.BlockSpec((CONV_SEQS, B_WIDTH), lambda i: (i, ga0)),
                  pl.BlockSpec((CONV_SEQS, B_WIDTH), lambda i: (i, ga0 + 1)),
                  pl.BlockSpec((HIST, CONV_SEQS, B_WIDTH), lambda i: (0, i, 0)),
                  pl.BlockSpec((CONV_W, B_WIDTH), lambda i: (0, 0)),
                  vec, vec, vec],
        out_specs=[pl.BlockSpec((CONV_SEQS, B_WIDTH), lambda i: (i, 0)),
                   pl.BlockSpec((HIST, CONV_SEQS, B_WIDTH), lambda i: (0, i, 0))],
        out_shape=[jax.ShapeDtypeStruct((nb, B_WIDTH), F32),
                   jax.ShapeDtypeStruct(state_t.shape, F32)],
        compiler_params=_params(_vmem_limit(blocks, 2 * 1024 * 1024), 1),
        name="conv_sample",
    )(z_small, z_small, state_t, conv_w, conv_b, gn_g, gn_b)


def _outproj_rows(oa_ref, ob_ref, x_ref, wa_ref, wb_ref, g_ref, h_ref, hf_ref):
    h = x_ref[...]
    h = h + jnp.dot(oa_ref[...].astype(BF16), wa_ref[...], preferred_element_type=F32)
    h = h + jnp.dot(ob_ref[...].astype(BF16), wb_ref[...], preferred_element_type=F32)
    h_ref[...] = h
    hf_ref[...] = _rms_rows(h, g_ref[...]).astype(BF16)


def _outproj_kernel(oa_ref, ob_ref, x_ref, oas_ref, obs_ref, xs_ref, wa_ref, wb_ref, g_ref,
                    h_ref, hf_ref, hs_ref, hfs_ref):
    _outproj_rows(oa_ref, ob_ref, x_ref, wa_ref, wb_ref, g_ref, h_ref, hf_ref)

    @pl.when(pl.program_id(0) == 0)
    def _():
        _outproj_rows(oas_ref, obs_ref, xs_ref, wa_ref, wb_ref, g_ref, hs_ref, hfs_ref)


def _outproj(oa, ob, x, oa_s, ob_s, x_s, w_bf16, g, tm):
    m, ms = x.shape[0], x_s.shape[0]
    blocks = (2 * tm * A_WIDTH * oa.dtype.itemsize + tm * D_MODEL * (4 + 4 + 2) + D_MODEL * D_MODEL * 2
              + 2 * ms * A_WIDTH * oa_s.dtype.itemsize + ms * D_MODEL * (4 + 4 + 2))
    whole = lambda cols: pl.BlockSpec((ms, cols), lambda i: (0, 0))
    return pl.pallas_call(
        _outproj_kernel,
        grid=(m // tm,),
        in_specs=[pl.BlockSpec((tm, A_WIDTH), lambda i: (i, 0)),
                  pl.BlockSpec((tm, B_WIDTH), lambda i: (i, 0)),
                  pl.BlockSpec((tm, D_MODEL), lambda i: (i, 0)),
                  whole(A_WIDTH), whole(B_WIDTH), whole(D_MODEL),
                  pl.BlockSpec((A_WIDTH, D_MODEL), lambda i: (0, 0)),
                  pl.BlockSpec((B_WIDTH, D_MODEL), lambda i: (1, 0)),
                  pl.BlockSpec((1, D_MODEL), lambda i: (0, 0))],
        out_specs=[pl.BlockSpec((tm, D_MODEL), lambda i: (i, 0)),
                   pl.BlockSpec((tm, D_MODEL), lambda i: (i, 0)),
                   whole(D_MODEL), whole(D_MODEL)],
        out_shape=[jax.ShapeDtypeStruct((m, D_MODEL), F32),
                   jax.ShapeDtypeStruct((m, D_MODEL), BF16),
                   jax.ShapeDtypeStruct((ms, D_MODEL), F32),
                   jax.ShapeDtypeStruct((ms, D_MODEL), BF16)],
        compiler_params=_params(_vmem_limit(blocks, 2 * tm * D_MODEL * 4), 1),
        name="outproj",
    )(oa, ob, x, oa_s, ob_s, x_s, w_bf16, w_bf16, g)


RESID_ROWS = 128


def _swiglu_down(hf, wg_ref, wu_ref, wd_ref):
    gate = jnp.dot(hf, wg_ref[...], preferred_element_type=F32)
    up = jnp.dot(hf, wu_ref[...], preferred_element_type=F32)
    return jnp.dot((_silu(gate) * up).astype(BF16), wd_ref[...], preferred_element_type=F32)


def _ffn_kernel(hf_ref, h_ref, hfs_ref, hs_ref, wg_ref, wu_ref, wd_ref, g_ref, y_ref, ys_ref):
    f = pl.program_id(1)
    last = pl.num_programs(1) - 1
    resid = h_ref.shape[0]

    @pl.when(f == 0)
    def _():
        y_ref[...] = jnp.zeros(y_ref.shape, F32)

    y_ref[...] += _swiglu_down(hf_ref[...], wg_ref, wu_ref, wd_ref)

    @pl.when(f < y_ref.shape[0] // resid)
    def _():
        rows = pl.ds(pl.multiple_of(f * resid, resid), resid)
        y_ref[rows, :] += h_ref[...]

    @pl.when(f == last)
    def _():
        y_ref[...] = _rms_rows(y_ref[...], g_ref[...])

    @pl.when(pl.program_id(0) == 0)
    def _():
        @pl.when(f == 0)
        def _():
            ys_ref[...] = hs_ref[...]

        ys_ref[...] += _swiglu_down(hfs_ref[...], wg_ref, wu_ref, wd_ref)

        @pl.when(f == last)
        def _():
            ys_ref[...] = _rms_rows(ys_ref[...], g_ref[...])


def _ffn(hf, h1, hf_s, h1_s, wg, wu, wd, g, tm, tf):
    m, ms = hf.shape[0], hf_s.shape[0]
    resid = min(RESID_ROWS, tm)
    slabs = tm // resid
    assert tm % resid == 0 and slabs <= D_FF // tf
    blocks = (tm * D_MODEL * (2 + 4) + resid * D_MODEL * 4 + 3 * D_MODEL * tf * 2
              + ms * D_MODEL * (2 + 4 + 4))
    resident = tm * D_MODEL * 4 + 3 * tm * tf * 4
    whole = lambda rows: pl.BlockSpec((rows, D_MODEL), lambda i, f: (0, 0))
    return pl.pallas_call(
        _ffn_kernel,
        grid=(m // tm, D_FF // tf),
        in_specs=[pl.BlockSpec((tm, D_MODEL), lambda i, f: (i, 0)),
                  pl.BlockSpec((resid, D_MODEL), lambda i, f: (i * slabs + jnp.minimum(f, slabs - 1), 0)),
                  whole(ms), whole(ms),
                  pl.BlockSpec((D_MODEL, tf), lambda i, f: (0, f)),
                  pl.BlockSpec((D_MODEL, tf), lambda i, f: (0, f)),
                  pl.BlockSpec((tf, D_MODEL), lambda i, f: (f, 0)),
                  pl.BlockSpec((1, D_MODEL), lambda i, f: (0, 0))],
        out_specs=[pl.BlockSpec((tm, D_MODEL), lambda i, f: (i, 0)), whole(ms)],
        out_shape=[jax.ShapeDtypeStruct((m, D_MODEL), F32),
                   jax.ShapeDtypeStruct((ms, D_MODEL), F32)],
        compiler_params=_params(_vmem_limit(blocks, resident), 2),
        name="ffn",
    )(hf, h1, hf_s, h1_s, wg, wu, wd, g)


def kernel(x_prompt, x_sample, state_hgrn, state_conv, meta_tokens, norm_mix_g, w_in, lb_logits,
           hgrn_norm_g, conv_w, conv_b, gn_g, gn_b, w_out, norm_ffn_g, w_ffn_gate, w_ffn_up,
           w_ffn_down, norm_final_g):
    batch, seq, _ = x_prompt.shape
    nb = x_sample.shape[0]
    assert x_sample.shape[1] == 1 and norm_mix_g.shape[0] == 1
    assert seq % (CHUNK * CHUNK_UNROLL) == 0 and seq % CONV_TILE == 0
    assert nb % N_META == 0 and nb % CONV_SEQS == 0 and nb % SAMPLE_ROWS == 0

    w_in_b = w_in[0].astype(BF16)
    g_final = norm_final_g[None, :]

    xp = x_prompt.reshape(batch * seq, D_MODEL)
    xs = x_sample.reshape(nb, D_MODEL)
    x_small = jnp.concatenate([xs, meta_tokens], axis=0)

    z, z_small = _inproj(xp, x_small, norm_mix_g, w_in_b, tm=1024, tn=1536)

    oa_p, s_p, w_out_b = _hgrn_prompt(z, z_small, lb_logits, hgrn_norm_g, w_out[0], batch, seq)
    ob_p, c_p, wg_b, wu_b, wd_b = _conv_prompt(z, z_small, conv_w[0], conv_b, gn_g, gn_b,
                                               w_ffn_gate[0], w_ffn_up[0], w_ffn_down[0], batch, seq)
    z_s = z_small[:nb].reshape(nb, 1, IN_COLS)
    oa_s, s_s = _hgrn_sample(z_s, state_hgrn[0], lb_logits, hgrn_norm_g)
    ob_s, c_s = _conv_sample(z_small, jnp.transpose(state_conv[0], (1, 0, 2)), conv_w[0], conv_b, gn_g, gn_b)
    c_s = jnp.transpose(c_s, (1, 0, 2))
    oa_s = oa_s.reshape(nb, A_WIDTH)

    h_p, hf_p, h_s, hf_s = _outproj(oa_p, ob_p, xp, oa_s, ob_s, xs, w_out_b, norm_ffn_g, tm=512)

    y_p, y_s = _ffn(hf_p, h_p, hf_s, h_s, wg_b, wu_b, wd_b, g_final, tm=1024, tf=512)

    return (y_p.reshape(batch, seq, D_MODEL), y_s.reshape(nb, 1, D_MODEL),
            s_p[None], c_p[None], s_s[None], c_s[None])
```

```python
import jax
import jax.numpy as jnp
from jax import lax
from jax.experimental import pallas as pl
from jax.experimental.pallas import tpu as pltpu

F32 = jnp.float32
BF16 = jnp.bfloat16

D_MODEL = 2048
N_META = 16
A_WIDTH = 1024
B_WIDTH = 1024
HEAD = 128
A_HEADS = A_WIDTH // HEAD
CONV_W = 31
HIST = CONV_W - 1
B_GROUPS = 8
GROUP = B_WIDTH // B_GROUPS
D_FF = 5632
IN_COLS = 4 * A_WIDTH + 2 * B_WIDTH
EPS = 1e-6

V7X_LANES = 128
V7X_SUBLANES = 8
V7X_VMEM_BYTES = 64 * 1024 * 1024
V7X_VMEM_USABLE = V7X_VMEM_BYTES - 8 * 1024 * 1024
BF16_ROWS = 2 * V7X_SUBLANES

CHUNK = 64
SUB = V7X_SUBLANES
NEG_BIG = -1e30
FAST_BLOCK = 32
SAFE_EXP = 86.0
LOG2_E = 1.4426950408889634
HEADS_PER_STEP = 4
CHUNK_UNROLL = 4
TIME_TILE = 512
SPILL_BYTES = 8 * 1024 * 1024

_NT = (((1,), (1,)), ((), ()))
_TN = (((0,), (0,)), ((), ()))


def _vmem_limit(pipelined_block_bytes, resident_bytes):
    est = 2 * pipelined_block_bytes + resident_bytes
    return int(min(max(est, 16 * 1024 * 1024), V7X_VMEM_USABLE))


def _params(vmem_bytes, ndims):
    return pltpu.CompilerParams(dimension_semantics=("arbitrary",) * ndims,
                                vmem_limit_bytes=vmem_bytes)


def _sigmoid(x):
    return 1.0 / (1.0 + jnp.exp2(x * (-LOG2_E)))


def _silu(x):
    return x * _sigmoid(x)


def _rms_rows(x, g):
    return x * lax.rsqrt(jnp.mean(x * x, axis=-1, keepdims=True) + EPS) * g


def _inproj_kernel(x_ref, xs_ref, g_ref, w_ref, z_ref, zs_ref, xn_ref, xsn_ref):
    i, j = pl.program_id(0), pl.program_id(1)

    @pl.when(j == 0)
    def _():
        xn_ref[...] = _rms_rows(x_ref[...], g_ref[...]).astype(BF16)

    z_ref[...] = jnp.dot(xn_ref[...], w_ref[...], preferred_element_type=F32)

    @pl.when(i == 0)
    def _():
        @pl.when(j == 0)
        def _():
            xsn_ref[...] = _rms_rows(xs_ref[...], g_ref[...]).astype(BF16)

        zs_ref[...] = jnp.dot(xsn_ref[...], w_ref[...], preferred_element_type=F32)


def _inproj(x, x_small, g, w_bf16, tm, tn):
    m, ms = x.shape[0], x_small.shape[0]
    n = w_bf16.shape[1]
    nj = n // tn
    blocks = (tm + ms) * D_MODEL * 4 + D_MODEL * tn * 2 + (tm + ms) * tn * 4
    resident = (tm + ms) * D_MODEL * 2 + tm * tn * 4
    return pl.pallas_call(
        _inproj_kernel,
        grid=(m // tm, nj),
        in_specs=[pl.BlockSpec((tm, D_MODEL), lambda i, j: (i, 0)),
                  pl.BlockSpec((ms, D_MODEL), lambda i, j: (0, 0)),
                  pl.BlockSpec((1, D_MODEL), lambda i, j: (0, 0)),
                  pl.BlockSpec((D_MODEL, tn), lambda i, j: (0, j))],
        out_specs=[pl.BlockSpec((tm, tn), lambda i, j: (i, j)),
                   pl.BlockSpec((ms, tn), lambda i, j: (0, jnp.where(i == 0, j, nj - 1)))],
        out_shape=[jax.ShapeDtypeStruct((m, n), F32), jax.ShapeDtypeStruct((ms, n), F32)],
        scratch_shapes=[pltpu.VMEM((tm, D_MODEL), BF16), pltpu.VMEM((ms, D_MODEL), BF16)],
        compiler_params=_params(_vmem_limit(blocks, resident), 2),
        name="inproj",
    )(x, x_small, g, w_bf16)


def _lower_bound(lb_logits):
    m = jnp.max(lb_logits, axis=0, keepdims=True)
    e = jnp.exp(lb_logits - m)
    return e[0:1, :] / jnp.sum(e, axis=0, keepdims=True)


def _cumsum_rows(x):
    n = x.shape[0]
    row = lax.broadcasted_iota(jnp.int32, x.shape, 0)
    s = 1
    while s < n:
        x = x + jnp.where(row >= s, pltpu.roll(x, s, 0), 0.0)
        s *= 2
    return x


def _gates(zf, lb):
    fg = lb + (1.0 - lb) * _sigmoid(zf)
    return 1.0 - fg, jnp.log(fg) * LOG2_E


def _chunk_masks(c, diag_block):
    row_w = lax.broadcasted_iota(jnp.int32, (c, HEAD), 0)
    row = lax.broadcasted_iota(jnp.int32, (c, c), 0)
    col = lax.broadcasted_iota(jnp.int32, (c, c), 1)
    levels = []
    half = c // 2
    while half >= diag_block:
        blk = 2 * half
        is_right = (row_w & (blk - 1)) >= half
        same_blk = (row ^ col) < blk
        valid = same_blk & ((row & (blk - 1)) >= half) & ((col & (blk - 1)) < half)
        levels.append((half, is_right, valid))
        half //= 2
    diag = ((row ^ col) < diag_block) & (col <= row)
    return levels, diag


def _block_rows(b, blk, offset):
    return jnp.concatenate(
        [jnp.broadcast_to(b[s0 + offset:s0 + offset + 1], (blk, HEAD)) for s0 in range(0, b.shape[0], blk)],
        axis=0)


def _diag_scores_exact(q, k, b):
    c = q.shape[0]
    row8 = lax.broadcasted_iota(jnp.int32, (SUB, HEAD), 0)
    lane = lax.broadcasted_iota(jnp.int32, (SUB, c), 1)
    diag = []
    for i in range(c // SUB):
        bi, qi, ki = (t[SUB * i:SUB * (i + 1)] for t in (b, q, k))
        blk = jnp.zeros((SUB, c), F32)
        for s in range(SUB):
            arg = jnp.where(row8 >= s, bi - bi[s:s + 1], NEG_BIG)
            a = jnp.sum(qi * ki[s:s + 1] * jnp.exp2(arg), axis=-1, keepdims=True)
            blk = jnp.where(lane == SUB * i + s, a, blk)
        diag.append(blk)
    return jnp.concatenate(diag, axis=0)


def _score_terms(q, k, b, masks, exact):
    levels, diag_mask = masks
    if exact:
        terms = [(None, _diag_scores_exact(q, k, b))]
    else:
        d = b - _block_rows(b, FAST_BLOCK, FAST_BLOCK // 2 - 1)
        e = jnp.exp2(d)
        xq = (q * e).astype(BF16)
        xk = (k / e).astype(BF16)
        terms = [(diag_mask, lax.dot_general(xq, xk, _NT, preferred_element_type=F32))]
    for half, is_right, valid in levels:
        mid = _block_rows(b, 2 * half, half - 1)
        x = (jnp.where(is_right, q, k) * jnp.exp2(-jnp.abs(b - mid))).astype(BF16)
        terms.append((valid, lax.dot_general(x, x, _NT, preferred_element_type=F32)))
    return terms


def _sum_terms(terms):
    sc = None
    for mask, p in terms:
        p = p if mask is None else jnp.where(mask, p, 0.0)
        sc = p if sc is None else sc + p
    return sc


def _chunk_state(k, v, b, st):
    b_last = b[b.shape[0] - 1:, :]
    kd = (k * jnp.exp2(b_last - b)).astype(BF16)
    return st * jnp.exp2(b_last) + lax.dot_general(v.astype(BF16), kd, _TN, preferred_element_type=F32)


def _hgrn_out(o, g, zog):
    return _rms_rows(o, g) * _silu(zog)


def _mixers_kernel(zq_ref, zf_ref, zi_ref, zog_ref, ga_ref, gb_ref, mf_ref, mi_ref, mga_ref, mgb_ref,
                   lbl_ref, hg_ref, cw_ref, cb_ref, gg_ref, gbias_ref, wo_ref, wg_ref, wu_ref, wd_ref,
                   oa_ref, s_ref, ob_ref, buf_ref, wo_out, wg_out, wu_out, wd_out, st_ref, u_ref):
    wo_out[...] = wo_ref[...].astype(BF16)
    wg_out[...] = wg_ref[...].astype(BF16)
    wu_out[...] = wu_ref[...].astype(BF16)
    wd_out[...] = wd_ref[...].astype(BF16)

    t = pl.program_id(2)
    tile = zq_ref.shape[0]
    n_chunks = tile // CHUNK
    heads = [slice(h * HEAD, (h + 1) * HEAD) for h in range(HEADS_PER_STEP)]
    lbs = [_lower_bound(lbl_ref[:, hs]) for hs in heads]

    @pl.when(t == 0)
    def _():
        for h, (hs, lb) in enumerate(zip(heads, lbs)):
            mk, mlf = _gates(mf_ref[:, hs], lb)
            st_ref[h] = _chunk_state(mk, mi_ref[:, hs], _cumsum_rows(mlf), jnp.zeros((HEAD, HEAD), F32))
            u_ref[h, 0:CONV_PAD - N_META, :] = jnp.zeros((CONV_PAD - N_META, GROUP), F32)
            u_ref[h, CONV_PAD - N_META:CONV_PAD, :] = mga_ref[:, hs] * _sigmoid(mgb_ref[:, hs])

    def scan(exact, sts):
        masks = _chunk_masks(CHUNK, SUB if exact else FAST_BLOCK)

        def body(c, sts):
            base = pl.multiple_of(c * CHUNK, CHUNK)
            rows = pl.ds(base, CHUNK)
            issued = []
            for hs, lb, st in zip(heads, lbs, sts):
                q = _silu(zq_ref[rows, hs])
                k, lf = _gates(zf_ref[rows, hs], lb)
                v = zi_ref[rows, hs].astype(BF16)
                b = _cumsum_rows(lf)
                o_inter = lax.dot_general((q * jnp.exp2(b)).astype(BF16), st.astype(BF16), _NT,
                                          preferred_element_type=F32)
                issued.append((v, o_inter, _score_terms(q, k, b, masks, exact), _chunk_state(k, v, b, st)))
            outs = [o_inter + jnp.dot(_sum_terms(terms).astype(BF16), v, preferred_element_type=F32)
                    for v, o_inter, terms, _ in issued]
            for hs, o in zip(heads, outs):
                oa_ref[rows, hs] = _hgrn_out(o, hg_ref[:, hs], zog_ref[rows, hs]).astype(oa_ref.dtype)
            for g, gs in enumerate(heads):
                u_ref[g, pl.ds(CONV_PAD + base, CHUNK), :] = ga_ref[rows, gs] * _sigmoid(gb_ref[rows, gs])
                acc = jnp.zeros((CHUNK, GROUP), F32) + cb_ref[:, gs]
                for j in range(CONV_W):
                    acc = acc + cw_ref[j:j + 1, gs] * u_ref[g, pl.ds(base + CONV_PAD - HIST + j, CHUNK), :]
                ob_ref[rows, gs] = _group_norm_swish(acc, gg_ref[:, gs], gbias_ref[:, gs]).astype(ob_ref.dtype)
            return tuple(new_st for _, _, _, new_st in issued)

        return lax.fori_loop(0, n_chunks, body, sts, unroll=CHUNK_UNROLL)

    min_log_lb = jnp.min(jnp.log(jnp.concatenate(lbs, axis=1))) * LOG2_E
    sts = lax.cond(min_log_lb * (FAST_BLOCK // 2) >= -SAFE_EXP,
                   lambda sts: scan(False, sts), lambda sts: scan(True, sts),
                   tuple(st_ref[h] for h in range(HEADS_PER_STEP)))
    for h, (hs, st) in enumerate(zip(heads, sts)):
        st_ref[h] = st
        s_ref[0, h] = st.T
        buf_ref[0, :, hs] = u_ref[h, CONV_PAD + tile - HIST:CONV_PAD + tile, :]
        u_ref[h, 0:CONV_PAD, :] = u_ref[h, tile:tile + CONV_PAD, :]


def _mixers(z, z_small, lb_logits, hgrn_g, conv_w, conv_b, gn_g, gn_b, w_out, w_gate, w_up, w_down, batch, seq):
    assert A_HEADS == B_GROUPS and HEAD == GROUP and seq % TIME_TILE == 0
    width = HEADS_PER_STEP * HEAD
    halves = A_WIDTH // width
    tiles = seq // TIME_TILE
    steps = batch * halves * tiles
    for w in (w_out, w_gate, w_up, w_down):
        assert w.shape[0] % (BF16_ROWS * steps) == 0
    step_of = lambda b, h, t: (b * halves + h) * tiles + t
    slab = lambda w: pl.BlockSpec((w.shape[0] // steps, w.shape[1]), lambda b, h, t: (step_of(b, h, t), 0))
    zcol = lambda section: pl.BlockSpec((TIME_TILE, width), lambda b, h, t: (b * tiles + t, section * halves + h))
    meta_blk = z_small.shape[0] // N_META - 1
    meta = lambda section: pl.BlockSpec((N_META, width), lambda b, h, t: (meta_blk, section * halves + h))
    vec = lambda rows: pl.BlockSpec((rows, width), lambda b, h, t: (0, h))
    tile_out = pl.BlockSpec((TIME_TILE, width), lambda b, h, t: (b * tiles + t, h))
    casts = (w_out, w_gate, w_up, w_down)
    blocks = (6 * TIME_TILE * width * 4 + 2 * TIME_TILE * width * 2 + 4 * N_META * width * 4
              + sum(w.shape[0] // steps * w.shape[1] for w in casts) * (4 + 2))
    resident = HEADS_PER_STEP * (HEAD * HEAD + (CONV_PAD + TIME_TILE) * GROUP) * 4 + SPILL_BYTES
    return pl.pallas_call(
        _mixers_kernel,
        grid=(batch, halves, tiles),
        in_specs=[zcol(0), zcol(1), zcol(2), zcol(3), zcol(4), zcol(5),
                  meta(1), meta(2), meta(4), meta(5),
                  vec(lb_logits.shape[0]), vec(1), vec(CONV_W), vec(1), vec(1), vec(1),
                  slab(w_out), slab(w_gate), slab(w_up), slab(w_down)],
        out_specs=[tile_out,
                   pl.BlockSpec((1, HEADS_PER_STEP, HEAD, HEAD), lambda b, h, t: (b, h, 0, 0)),
                   tile_out,
                   pl.BlockSpec((1, HIST, width), lambda b, h, t: (b, 0, h)),
                   slab(w_out), slab(w_gate), slab(w_up), slab(w_down)],
        out_shape=[jax.ShapeDtypeStruct((batch * seq, A_WIDTH), BF16),
                   jax.ShapeDtypeStruct((batch, A_HEADS, HEAD, HEAD), F32),
                   jax.ShapeDtypeStruct((batch * seq, B_WIDTH), BF16),
                   jax.ShapeDtypeStruct((batch, HIST, B_WIDTH), F32)]
                  + [jax.ShapeDtypeStruct(w.shape, BF16) for w in casts],
        scratch_shapes=[pltpu.VMEM((HEADS_PER_STEP, HEAD, HEAD), F32),
                        pltpu.VMEM((HEADS_PER_STEP, CONV_PAD + TIME_TILE, GROUP), F32)],
        compiler_params=_params(_vmem_limit(blocks, resident), 3),
        name="mixers",
    )(z, z, z, z, z, z, z_small, z_small, z_small, z_small, lb_logits, hgrn_g,
      conv_w, conv_b, gn_g, gn_b, w_out, w_gate, w_up, w_down)


SAMPLE_ROWS = 16
SAMPLE_UNROLL = 4
SPLAT_ROWS = BF16_ROWS


def _hgrn_sample_kernel(z_ref, s_ref, lbl_ref, g_ref, o_ref, so_ref):
    row = lax.broadcasted_iota(jnp.int32, (SPLAT_ROWS, HEAD), 0)
    ones_f = jnp.where(row < 3, 1.0, 0.0)
    ones_q = jnp.where(row == 4, 1.0, 0.0)

    def per_seq(r, carry):
        zrow = z_ref[r]
        splats = []
        for h in range(A_HEADS):
            sl = lambda base: zrow[:, base * A_WIDTH + h * HEAD: base * A_WIDTH + (h + 1) * HEAD]
            lb = _lower_bound(lbl_ref[:, h * HEAD:(h + 1) * HEAD])
            q = _silu(sl(0))
            fg = lb + (1.0 - lb) * _sigmoid(sl(1))
            k = 1.0 - fg
            v = sl(2)
            f_hi = fg.astype(BF16).astype(F32)
            f_mid = (fg - f_hi).astype(BF16).astype(F32)
            f_lo = fg - f_hi - f_mid
            a = jnp.where(row == 0, f_hi, jnp.where(row == 1, f_mid, jnp.where(
                row == 2, f_lo, jnp.where(row == 3, k, jnp.where(row == 4, q, 0.0)))))
            bmat = jnp.concatenate([ones_f, ones_q, jnp.where(row == 3, v, 0.0)], axis=1)
            splats.append(lax.dot_general(a.astype(BF16), bmat.astype(BF16), _TN,
                                          preferred_element_type=F32))
        for h, gm in enumerate(splats):
            s_new = gm[:, 0:HEAD] * s_ref[r, h] + gm[:, 2 * HEAD:3 * HEAD]
            so_ref[r, h] = s_new
            o = jnp.sum(gm[:, HEAD:2 * HEAD] * s_new, axis=0, keepdims=True)
            og = zrow[:, 3 * A_WIDTH + h * HEAD: 3 * A_WIDTH + (h + 1) * HEAD]
            o_ref[r, :, h * HEAD:(h + 1) * HEAD] = _hgrn_out(
                o, g_ref[:, h * HEAD:(h + 1) * HEAD], og)
        return carry

    lax.fori_loop(0, SAMPLE_ROWS, per_seq, 0, unroll=SAMPLE_UNROLL)


def _hgrn_sample(z_small, state, lb_logits, hgrn_g):
    nb = state.shape[0]
    blocks = SAMPLE_ROWS * (IN_COLS * 4 + A_WIDTH * 4 + 2 * A_HEADS * HEAD * HEAD * 4)
    return pl.pallas_call(
        _hgrn_sample_kernel,
        grid=(nb // SAMPLE_ROWS,),
        in_specs=[pl.BlockSpec((SAMPLE_ROWS, 1, IN_COLS), lambda i: (i, 0, 0)),
                  pl.BlockSpec((SAMPLE_ROWS, A_HEADS, HEAD, HEAD), lambda i: (i, 0, 0, 0)),
                  pl.BlockSpec(lb_logits.shape, lambda i: (0, 0)),
                  pl.BlockSpec((1, A_WIDTH), lambda i: (0, 0))],
        out_specs=[pl.BlockSpec((SAMPLE_ROWS, 1, A_WIDTH), lambda i: (i, 0, 0)),
                   pl.BlockSpec((SAMPLE_ROWS, A_HEADS, HEAD, HEAD), lambda i: (i, 0, 0, 0))],
        out_shape=[jax.ShapeDtypeStruct((nb, 1, A_WIDTH), F32),
                   jax.ShapeDtypeStruct(state.shape, F32)],
        compiler_params=_params(_vmem_limit(blocks, 2 * 1024 * 1024), 1),
        name="hgrn_sample",
    )(z_small, state, lb_logits, hgrn_g)


def _group_norm_swish(c, gn_g, gn_b):
    mu = jnp.mean(c, axis=-1, keepdims=True)
    d = c - mu
    var = jnp.mean(d * d, axis=-1, keepdims=True)
    return _silu(d * lax.rsqrt(var + EPS) * gn_g + gn_b)


CONV_PAD = 32


CONV_SEQS = 16


def _conv_sample_kernel(ga_ref, gb_ref, st_ref, w_ref, cb_ref, gg_ref, gbias_ref, o_ref, buf_ref):
    u = ga_ref[...] * _sigmoid(gb_ref[...])
    c = w_ref[HIST:CONV_W, :] * u + cb_ref[...]
    for j in range(HIST):
        c = c + w_ref[j:j + 1, :] * st_ref[j]
    for g in range(B_GROUPS):
        sl = slice(g * GROUP, (g + 1) * GROUP)
        o_ref[:, sl] = _group_norm_swish(c[:, sl], gg_ref[:, sl], gbias_ref[:, sl])
    for j in range(HIST - 1):
        buf_ref[j] = st_ref[j + 1]
    buf_ref[HIST - 1] = u


def _conv_sample(z_small, state_t, conv_w, conv_b, gn_g, gn_b):
    nb = state_t.shape[1]
    ga0 = 4 * A_WIDTH // B_WIDTH
    vec = pl.BlockSpec((1, B_WIDTH), lambda i: (0, 0))
    blocks = CONV_SEQS * B_WIDTH * 4 * (3 + 2 * HIST) + 32 * B_WIDTH * 4
    return pl.pallas_call(
        _conv_sample_kernel,
        grid=(nb // CONV_SEQS,),
        in_specs=[pl.BlockSpec((CONV_SEQS, B_WIDTH), lambda i: (i, ga0)),
                  pl.BlockSpec((CONV_SEQS, B_WIDTH), lambda i: (i, ga0 + 1)),
                  pl.BlockSpec((HIST, CONV_SEQS, B_WIDTH), lambda i: (0, i, 0)),
                  pl.BlockSpec((CONV_W, B_WIDTH), lambda i: (0, 0)),
                  vec, vec, vec],
        out_specs=[pl.BlockSpec((CONV_SEQS, B_WIDTH), lambda i: (i, 0)),
                   pl.BlockSpec((HIST, CONV_SEQS, B_WIDTH), lambda i: (0, i, 0))],
        out_shape=[jax.ShapeDtypeStruct((nb, B_WIDTH), F32),
                   jax.ShapeDtypeStruct(state_t.shape, F32)],
        compiler_params=_params(_vmem_limit(blocks, 2 * 1024 * 1024), 1),
        name="conv_sample",
    )(z_small, z_small, state_t, conv_w, conv_b, gn_g, gn_b)


def _outproj_rows(oa_ref, ob_ref, x_ref, wa_ref, wb_ref, g_ref, h_ref, hf_ref):
    h = x_ref[...]
    h = h + jnp.dot(oa_ref[...].astype(BF16), wa_ref[...], preferred_element_type=F32)
    h = h + jnp.dot(ob_ref[...].astype(BF16), wb_ref[...], preferred_element_type=F32)
    h_ref[...] = h
    hf_ref[...] = _rms_rows(h, g_ref[...]).astype(BF16)


def _outproj_kernel(oa_ref, ob_ref, x_ref, oas_ref, obs_ref, xs_ref, wa_ref, wb_ref, g_ref,
                    h_ref, hf_ref, hs_ref, hfs_ref):
    _outproj_rows(oa_ref, ob_ref, x_ref, wa_ref, wb_ref, g_ref, h_ref, hf_ref)

    @pl.when(pl.program_id(0) == 0)
    def _():
        _outproj_rows(oas_ref, obs_ref, xs_ref, wa_ref, wb_ref, g_ref, hs_ref, hfs_ref)


def _outproj(oa, ob, x, oa_s, ob_s, x_s, w_bf16, g, tm):
    m, ms = x.shape[0], x_s.shape[0]
    blocks = (2 * tm * A_WIDTH * oa.dtype.itemsize + tm * D_MODEL * (4 + 4 + 2) + D_MODEL * D_MODEL * 2
              + 2 * ms * A_WIDTH * oa_s.dtype.itemsize + ms * D_MODEL * (4 + 4 + 2))
    whole = lambda cols: pl.BlockSpec((ms, cols), lambda i: (0, 0))
    return pl.pallas_call(
        _outproj_kernel,
        grid=(m // tm,),
        in_specs=[pl.BlockSpec((tm, A_WIDTH), lambda i: (i, 0)),
                  pl.BlockSpec((tm, B_WIDTH), lambda i: (i, 0)),
                  pl.BlockSpec((tm, D_MODEL), lambda i: (i, 0)),
                  whole(A_WIDTH), whole(B_WIDTH), whole(D_MODEL),
                  pl.BlockSpec((A_WIDTH, D_MODEL), lambda i: (0, 0)),
                  pl.BlockSpec((B_WIDTH, D_MODEL), lambda i: (1, 0)),
                  pl.BlockSpec((1, D_MODEL), lambda i: (0, 0))],
        out_specs=[pl.BlockSpec((tm, D_MODEL), lambda i: (i, 0)),
                   pl.BlockSpec((tm, D_MODEL), lambda i: (i, 0)),
                   whole(D_MODEL), whole(D_MODEL)],
        out_shape=[jax.ShapeDtypeStruct((m, D_MODEL), F32),
                   jax.ShapeDtypeStruct((m, D_MODEL), BF16),
                   jax.ShapeDtypeStruct((ms, D_MODEL), F32),
                   jax.ShapeDtypeStruct((ms, D_MODEL), BF16)],
        compiler_params=_params(_vmem_limit(blocks, 2 * tm * D_MODEL * 4), 1),
        name="outproj",
    )(oa, ob, x, oa_s, ob_s, x_s, w_bf16, w_bf16, g)


RESID_ROWS = 128


def _swiglu_down(hf, wg_ref, wu_ref, wd_ref):
    gate = jnp.dot(hf, wg_ref[...], preferred_element_type=F32)
    up = jnp.dot(hf, wu_ref[...], preferred_element_type=F32)
    return jnp.dot((_silu(gate) * up).astype(BF16), wd_ref[...], preferred_element_type=F32)


def _ffn_kernel(hf_ref, h_ref, hfs_ref, hs_ref, wg_ref, wu_ref, wd_ref, g_ref, y_ref, ys_ref):
    f = pl.program_id(1)
    last = pl.num_programs(1) - 1
    resid = h_ref.shape[0]

    @pl.when(f == 0)
    def _():
        y_ref[...] = jnp.zeros(y_ref.shape, F32)

    y_ref[...] += _swiglu_down(hf_ref[...], wg_ref, wu_ref, wd_ref)

    @pl.when(f < y_ref.shape[0] // resid)
    def _():
        rows = pl.ds(pl.multiple_of(f * resid, resid), resid)
        y_ref[rows, :] += h_ref[...]

    @pl.when(f == last)
    def _():
        y_ref[...] = _rms_rows(y_ref[...], g_ref[...])

    @pl.when(pl.program_id(0) == 0)
    def _():
        @pl.when(f == 0)
        def _():
            ys_ref[...] = hs_ref[...]

        ys_ref[...] += _swiglu_down(hfs_ref[...], wg_ref, wu_ref, wd_ref)

        @pl.when(f == last)
        def _():
            ys_ref[...] = _rms_rows(ys_ref[...], g_ref[...])


def _ffn(hf, h1, hf_s, h1_s, wg, wu, wd, g, tm, tf):
    m, ms = hf.shape[0], hf_s.shape[0]
    resid = min(RESID_ROWS, tm)
    slabs = tm // resid
    assert tm % resid == 0 and slabs <= D_FF // tf
    blocks = (tm * D_MODEL * (2 + 4) + resid * D_MODEL * 4 + 3 * D_MODEL * tf * 2
              + ms * D_MODEL * (2 + 4 + 4))
    resident = tm * D_MODEL * 4 + 3 * tm * tf * 4
    whole = lambda rows: pl.BlockSpec((rows, D_MODEL), lambda i, f: (0, 0))
    return pl.pallas_call(
        _ffn_kernel,
        grid=(m // tm, D_FF // tf),
        in_specs=[pl.BlockSpec((tm, D_MODEL), lambda i, f: (i, 0)),
                  pl.BlockSpec((resid, D_MODEL), lambda i, f: (i * slabs + jnp.minimum(f, slabs - 1), 0)),
                  whole(ms), whole(ms),
                  pl.BlockSpec((D_MODEL, tf), lambda i, f: (0, f)),
                  pl.BlockSpec((D_MODEL, tf), lambda i, f: (0, f)),
                  pl.BlockSpec((tf, D_MODEL), lambda i, f: (f, 0)),
                  pl.BlockSpec((1, D_MODEL), lambda i, f: (0, 0))],
        out_specs=[pl.BlockSpec((tm, D_MODEL), lambda i, f: (i, 0)), whole(ms)],
        out_shape=[jax.ShapeDtypeStruct((m, D_MODEL), F32),
                   jax.ShapeDtypeStruct((ms, D_MODEL), F32)],
        compiler_params=_params(_vmem_limit(blocks, resident), 2),
        name="ffn",
    )(hf, h1, hf_s, h1_s, wg, wu, wd, g)


def kernel(x_prompt, x_sample, state_hgrn, state_conv, meta_tokens, norm_mix_g, w_in, lb_logits,
           hgrn_norm_g, conv_w, conv_b, gn_g, gn_b, w_out, norm_ffn_g, w_ffn_gate, w_ffn_up,
           w_ffn_down, norm_final_g):
    batch, seq, _ = x_prompt.shape
    nb = x_sample.shape[0]
    assert x_sample.shape[1] == 1 and norm_mix_g.shape[0] == 1
    assert TIME_TILE % (CHUNK * CHUNK_UNROLL) == 0
    assert nb % N_META == 0 and nb % CONV_SEQS == 0 and nb % SAMPLE_ROWS == 0

    w_in_b = w_in[0].astype(BF16)
    g_final = norm_final_g[None, :]

    xp = x_prompt.reshape(batch * seq, D_MODEL)
    xs = x_sample.reshape(nb, D_MODEL)
    x_small = jnp.concatenate([xs, meta_tokens], axis=0)

    z, z_small = _inproj(xp, x_small, norm_mix_g, w_in_b, tm=1024, tn=1536)

    oa_p, s_p, ob_p, c_p, w_out_b, wg_b, wu_b, wd_b = _mixers(
        z, z_small, lb_logits, hgrn_norm_g, conv_w[0], conv_b, gn_g, gn_b,
        w_out[0], w_ffn_gate[0], w_ffn_up[0], w_ffn_down[0], batch, seq)
    z_s = z_small[:nb].reshape(nb, 1, IN_COLS)
    oa_s, s_s = _hgrn_sample(z_s, state_hgrn[0], lb_logits, hgrn_norm_g)
    ob_s, c_s = _conv_sample(z_small, jnp.transpose(state_conv[0], (1, 0, 2)), conv_w[0], conv_b, gn_g, gn_b)
    c_s = jnp.transpose(c_s, (1, 0, 2))
    oa_s = oa_s.reshape(nb, A_WIDTH)

    h_p, hf_p, h_s, hf_s = _outproj(oa_p, ob_p, xp, oa_s, ob_s, xs, w_out_b, norm_ffn_g, tm=512)

    y_p, y_s = _ffn(hf_p, h_p, hf_s, h_s, wg_b, wu_b, wd_b, g_final, tm=1024, tf=512)

    return (y_p.reshape(batch, seq, D_MODEL), y_s.reshape(nb, 1, D_MODEL),
            s_p[None], c_p[None], s_s[None], c_s[None])
```

```python
import jax
import jax.numpy as jnp
from jax import lax
from jax.experimental import pallas as pl
from jax.experimental.pallas import tpu as pltpu

F32 = jnp.float32
BF16 = jnp.bfloat16

D_MODEL = 2048
N_META = 16
A_WIDTH = 1024
B_WIDTH = 1024
HEAD = 128
A_HEADS = A_WIDTH // HEAD
CONV_W = 31
HIST = CONV_W - 1
B_GROUPS = 8
GROUP = B_WIDTH // B_GROUPS
D_FF = 5632
IN_COLS = 4 * A_WIDTH + 2 * B_WIDTH
EPS = 1e-6

V7X_LANES = 128
V7X_SUBLANES = 8
V7X_VMEM_BYTES = 64 * 1024 * 1024
V7X_VMEM_USABLE = V7X_VMEM_BYTES - 8 * 1024 * 1024
BF16_ROWS = 2 * V7X_SUBLANES

CHUNK = 64
SUB = V7X_SUBLANES
NEG_BIG = -1e30
FAST_BLOCK = 32
SAFE_EXP = 86.0
LOG2_E = 1.4426950408889634
HEADS_PER_STEP = 4
CHUNK_UNROLL = 4
TIME_TILE = 512
SPILL_BYTES = 8 * 1024 * 1024

_NT = (((1,), (1,)), ((), ()))
_TN = (((0,), (0,)), ((), ()))


def _vmem_limit(pipelined_block_bytes, resident_bytes):
    est = 2 * pipelined_block_bytes + resident_bytes
    return int(min(max(est, 16 * 1024 * 1024), V7X_VMEM_USABLE))


def _params(vmem_bytes, ndims):
    return pltpu.CompilerParams(dimension_semantics=("arbitrary",) * ndims,
                                vmem_limit_bytes=vmem_bytes)


def _sigmoid(x):
    return 1.0 / (1.0 + jnp.exp2(x * (-LOG2_E)))


def _silu(x):
    return x * _sigmoid(x)


def _rms_rows(x, g):
    return x * lax.rsqrt(jnp.mean(x * x, axis=-1, keepdims=True) + EPS) * g


def _inproj_kernel(x_ref, xs_ref, g_ref, w_ref, z_ref, zs_ref, xn_ref, xsn_ref):
    i, j = pl.program_id(0), pl.program_id(1)

    @pl.when(j == 0)
    def _():
        xn_ref[...] = _rms_rows(x_ref[...], g_ref[...]).astype(BF16)

    z_ref[...] = jnp.dot(xn_ref[...], w_ref[...], preferred_element_type=F32)

    @pl.when(i == 0)
    def _():
        @pl.when(j == 0)
        def _():
            xsn_ref[...] = _rms_rows(xs_ref[...], g_ref[...]).astype(BF16)

        zs_ref[...] = jnp.dot(xsn_ref[...], w_ref[...], preferred_element_type=F32)


def _inproj(x, x_small, g, w_bf16, tm, tn):
    m, ms = x.shape[0], x_small.shape[0]
    n = w_bf16.shape[1]
    nj = n // tn
    blocks = (tm + ms) * D_MODEL * 4 + D_MODEL * tn * 2 + (tm + ms) * tn * 4
    resident = (tm + ms) * D_MODEL * 2 + tm * tn * 4
    return pl.pallas_call(
        _inproj_kernel,
        grid=(m // tm, nj),
        in_specs=[pl.BlockSpec((tm, D_MODEL), lambda i, j: (i, 0)),
                  pl.BlockSpec((ms, D_MODEL), lambda i, j: (0, 0)),
                  pl.BlockSpec((1, D_MODEL), lambda i, j: (0, 0)),
                  pl.BlockSpec((D_MODEL, tn), lambda i, j: (0, j))],
        out_specs=[pl.BlockSpec((tm, tn), lambda i, j: (i, j)),
                   pl.BlockSpec((ms, tn), lambda i, j: (0, jnp.where(i == 0, j, nj - 1)))],
        out_shape=[jax.ShapeDtypeStruct((m, n), F32), jax.ShapeDtypeStruct((ms, n), F32)],
        scratch_shapes=[pltpu.VMEM((tm, D_MODEL), BF16), pltpu.VMEM((ms, D_MODEL), BF16)],
        compiler_params=_params(_vmem_limit(blocks, resident), 2),
        name="inproj",
    )(x, x_small, g, w_bf16)


def _lower_bound(lb_logits):
    m = jnp.max(lb_logits, axis=0, keepdims=True)
    e = jnp.exp(lb_logits - m)
    return e[0:1, :] / jnp.sum(e, axis=0, keepdims=True)


def _cumsum_rows(x):
    n = x.shape[0]
    row = lax.broadcasted_iota(jnp.int32, x.shape, 0)
    s = 1
    while s < n:
        x = x + jnp.where(row >= s, pltpu.roll(x, s, 0), 0.0)
        s *= 2
    return x


def _gates(zf, lb):
    fg = lb + (1.0 - lb) * _sigmoid(zf)
    return 1.0 - fg, jnp.log(fg) * LOG2_E


def _chunk_masks(c, diag_block):
    row_w = lax.broadcasted_iota(jnp.int32, (c, HEAD), 0)
    row = lax.broadcasted_iota(jnp.int32, (c, c), 0)
    col = lax.broadcasted_iota(jnp.int32, (c, c), 1)
    levels = []
    half = c // 2
    while half >= diag_block:
        blk = 2 * half
        is_right = (row_w & (blk - 1)) >= half
        same_blk = (row ^ col) < blk
        valid = same_blk & ((row & (blk - 1)) >= half) & ((col & (blk - 1)) < half)
        levels.append((half, is_right, valid))
        half //= 2
    diag = ((row ^ col) < diag_block) & (col <= row)
    return levels, diag


def _block_rows(b, blk, offset):
    return jnp.concatenate(
        [jnp.broadcast_to(b[s0 + offset:s0 + offset + 1], (blk, HEAD)) for s0 in range(0, b.shape[0], blk)],
        axis=0)


def _diag_scores_exact(q, k, b):
    c = q.shape[0]
    row8 = lax.broadcasted_iota(jnp.int32, (SUB, HEAD), 0)
    lane = lax.broadcasted_iota(jnp.int32, (SUB, c), 1)
    diag = []
    for i in range(c // SUB):
        bi, qi, ki = (t[SUB * i:SUB * (i + 1)] for t in (b, q, k))
        blk = jnp.zeros((SUB, c), F32)
        for s in range(SUB):
            arg = jnp.where(row8 >= s, bi - bi[s:s + 1], NEG_BIG)
            a = jnp.sum(qi * ki[s:s + 1] * jnp.exp2(arg), axis=-1, keepdims=True)
            blk = jnp.where(lane == SUB * i + s, a, blk)
        diag.append(blk)
    return jnp.concatenate(diag, axis=0)


def _score_terms(q, k, b, masks, exact):
    levels, diag_mask = masks
    if exact:
        terms = [(None, _diag_scores_exact(q, k, b))]
    else:
        d = b - _block_rows(b, FAST_BLOCK, FAST_BLOCK // 2 - 1)
        e = jnp.exp2(d)
        xq = (q * e).astype(BF16)
        xk = (k / e).astype(BF16)
        terms = [(diag_mask, lax.dot_general(xq, xk, _NT, preferred_element_type=F32))]
    for half, is_right, valid in levels:
        mid = _block_rows(b, 2 * half, half - 1)
        x = (jnp.where(is_right, q, k) * jnp.exp2(-jnp.abs(b - mid))).astype(BF16)
        terms.append((valid, lax.dot_general(x, x, _NT, preferred_element_type=F32)))
    return terms


def _sum_terms(terms):
    sc = None
    for mask, p in terms:
        p = p if mask is None else jnp.where(mask, p, 0.0)
        sc = p if sc is None else sc + p
    return sc


def _chunk_state(k, v, b, st):
    b_last = b[b.shape[0] - 1:, :]
    kd = (k * jnp.exp2(b_last - b)).astype(BF16)
    return st * jnp.exp2(b_last) + lax.dot_general(v.astype(BF16), kd, _TN, preferred_element_type=F32)


def _hgrn_out(o, g, zog):
    return _rms_rows(o, g) * _silu(zog)


def _mixers_kernel(zq_ref, zf_ref, zi_ref, zog_ref, ga_ref, gb_ref, mf_ref, mi_ref, mga_ref, mgb_ref,
                   lbl_ref, hg_ref, cw_ref, cb_ref, gg_ref, gbias_ref, wo_ref, wg_ref, wu_ref, wd_ref,
                   oa_ref, s_ref, ob_ref, buf_ref, wo_out, wg_out, wu_out, wd_out, st_ref, u_ref):
    wo_out[...] = wo_ref[...].astype(BF16)
    wg_out[...] = wg_ref[...].astype(BF16)
    wu_out[...] = wu_ref[...].astype(BF16)
    wd_out[...] = wd_ref[...].astype(BF16)

    t = pl.program_id(2)
    tile = zq_ref.shape[0]
    n_chunks = tile // CHUNK
    heads = [slice(h * HEAD, (h + 1) * HEAD) for h in range(HEADS_PER_STEP)]
    lbs = [_lower_bound(lbl_ref[:, hs]) for hs in heads]

    @pl.when(t == 0)
    def _():
        for h, (hs, lb) in enumerate(zip(heads, lbs)):
            mk, mlf = _gates(mf_ref[:, hs], lb)
            st_ref[h] = _chunk_state(mk, mi_ref[:, hs], _cumsum_rows(mlf), jnp.zeros((HEAD, HEAD), F32))
            u_ref[h, 0:CONV_PAD - N_META, :] = jnp.zeros((CONV_PAD - N_META, GROUP), F32)
            u_ref[h, CONV_PAD - N_META:CONV_PAD, :] = mga_ref[:, hs] * _sigmoid(mgb_ref[:, hs])

    def scan(exact, sts):
        masks = _chunk_masks(CHUNK, SUB if exact else FAST_BLOCK)

        def body(c, sts):
            base = pl.multiple_of(c * CHUNK, CHUNK)
            rows = pl.ds(base, CHUNK)
            issued = []
            for hs, lb, st in zip(heads, lbs, sts):
                q = _silu(zq_ref[rows, hs])
                k, lf = _gates(zf_ref[rows, hs], lb)
                v = zi_ref[rows, hs].astype(BF16)
                b = _cumsum_rows(lf)
                o_inter = lax.dot_general((q * jnp.exp2(b)).astype(BF16), st.astype(BF16), _NT,
                                          preferred_element_type=F32)
                issued.append((v, o_inter, _score_terms(q, k, b, masks, exact), _chunk_state(k, v, b, st)))
            for g, gs in enumerate(heads):
                u_ref[g, pl.ds(CONV_PAD + base, CHUNK), :] = ga_ref[rows, gs] * _sigmoid(gb_ref[rows, gs])
                acc = jnp.zeros((CHUNK, GROUP), F32) + cb_ref[:, gs]
                for j in range(CONV_W):
                    acc = acc + cw_ref[j:j + 1, gs] * u_ref[g, pl.ds(base + CONV_PAD - HIST + j, CHUNK), :]
                ob_ref[rows, gs] = _group_norm_swish(acc, gg_ref[:, gs], gbias_ref[:, gs]).astype(ob_ref.dtype)
            outs = [o_inter + jnp.dot(_sum_terms(terms).astype(BF16), v, preferred_element_type=F32)
                    for v, o_inter, terms, _ in issued]
            for hs, o in zip(heads, outs):
                oa_ref[rows, hs] = _hgrn_out(o, hg_ref[:, hs], zog_ref[rows, hs]).astype(oa_ref.dtype)
            return tuple(new_st for _, _, _, new_st in issued)

        return lax.fori_loop(0, n_chunks, body, sts, unroll=CHUNK_UNROLL)

    min_log_lb = jnp.min(jnp.log(jnp.concatenate(lbs, axis=1))) * LOG2_E
    sts = lax.cond(min_log_lb * (FAST_BLOCK // 2) >= -SAFE_EXP,
                   lambda sts: scan(False, sts), lambda sts: scan(True, sts),
                   tuple(st_ref[h] for h in range(HEADS_PER_STEP)))
    for h, (hs, st) in enumerate(zip(heads, sts)):
        st_ref[h] = st
        s_ref[0, h] = st.T
        buf_ref[0, :, hs] = u_ref[h, CONV_PAD + tile - HIST:CONV_PAD + tile, :]
        u_ref[h, 0:CONV_PAD, :] = u_ref[h, tile:tile + CONV_PAD, :]


def _mixers(z, z_small, lb_logits, hgrn_g, conv_w, conv_b, gn_g, gn_b, w_out, w_gate, w_up, w_down, batch, seq):
    assert A_HEADS == B_GROUPS and HEAD == GROUP and seq % TIME_TILE == 0
    width = HEADS_PER_STEP * HEAD
    halves = A_WIDTH // width
    tiles = seq // TIME_TILE
    steps = batch * halves * tiles
    for w in (w_out, w_gate, w_up, w_down):
        assert w.shape[0] % (BF16_ROWS * steps) == 0
    step_of = lambda b, h, t: (b * halves + h) * tiles + t
    slab = lambda w: pl.BlockSpec((w.shape[0] // steps, w.shape[1]), lambda b, h, t: (step_of(b, h, t), 0))
    zcol = lambda section: pl.BlockSpec((TIME_TILE, width), lambda b, h, t: (b * tiles + t, section * halves + h))
    meta_blk = z_small.shape[0] // N_META - 1
    meta = lambda section: pl.BlockSpec((N_META, width), lambda b, h, t: (meta_blk, section * halves + h))
    vec = lambda rows: pl.BlockSpec((rows, width), lambda b, h, t: (0, h))
    tile_out = pl.BlockSpec((TIME_TILE, width), lambda b, h, t: (b * tiles + t, h))
    casts = (w_out, w_gate, w_up, w_down)
    blocks = (6 * TIME_TILE * width * 4 + 2 * TIME_TILE * width * 2 + 4 * N_META * width * 4
              + sum(w.shape[0] // steps * w.shape[1] for w in casts) * (4 + 2))
    resident = HEADS_PER_STEP * (HEAD * HEAD + (CONV_PAD + TIME_TILE) * GROUP) * 4 + SPILL_BYTES
    return pl.pallas_call(
        _mixers_kernel,
        grid=(batch, halves, tiles),
        in_specs=[zcol(0), zcol(1), zcol(2), zcol(3), zcol(4), zcol(5),
                  meta(1), meta(2), meta(4), meta(5),
                  vec(lb_logits.shape[0]), vec(1), vec(CONV_W), vec(1), vec(1), vec(1),
                  slab(w_out), slab(w_gate), slab(w_up), slab(w_down)],
        out_specs=[tile_out,
                   pl.BlockSpec((1, HEADS_PER_STEP, HEAD, HEAD), lambda b, h, t: (b, h, 0, 0)),
                   tile_out,
                   pl.BlockSpec((1, HIST, width), lambda b, h, t: (b, 0, h)),
                   slab(w_out), slab(w_gate), slab(w_up), slab(w_down)],
        out_shape=[jax.ShapeDtypeStruct((batch * seq, A_WIDTH), BF16),
                   jax.ShapeDtypeStruct((batch, A_HEADS, HEAD, HEAD), F32),
                   jax.ShapeDtypeStruct((batch * seq, B_WIDTH), BF16),
                   jax.ShapeDtypeStruct((batch, HIST, B_WIDTH), F32)]
                  + [jax.ShapeDtypeStruct(w.shape, BF16) for w in casts],
        scratch_shapes=[pltpu.VMEM((HEADS_PER_STEP, HEAD, HEAD), F32),
                        pltpu.VMEM((HEADS_PER_STEP, CONV_PAD + TIME_TILE, GROUP), F32)],
        compiler_params=_params(_vmem_limit(blocks, resident), 3),
        name="mixers",
    )(z, z, z, z, z, z, z_small, z_small, z_small, z_small, lb_logits, hgrn_g,
      conv_w, conv_b, gn_g, gn_b, w_out, w_gate, w_up, w_down)


SAMPLE_ROWS = 16
SAMPLE_UNROLL = 4
SPLAT_ROWS = BF16_ROWS


def _hgrn_sample_kernel(z_ref, s_ref, lbl_ref, g_ref, o_ref, so_ref):
    row = lax.broadcasted_iota(jnp.int32, (SPLAT_ROWS, HEAD), 0)
    ones_f = jnp.where(row < 3, 1.0, 0.0)
    ones_q = jnp.where(row == 4, 1.0, 0.0)

    def per_seq(r, carry):
        zrow = z_ref[r]
        splats = []
        for h in range(A_HEADS):
            sl = lambda base: zrow[:, base * A_WIDTH + h * HEAD: base * A_WIDTH + (h + 1) * HEAD]
            lb = _lower_bound(lbl_ref[:, h * HEAD:(h + 1) * HEAD])
            q = _silu(sl(0))
            fg = lb + (1.0 - lb) * _sigmoid(sl(1))
            k = 1.0 - fg
            v = sl(2)
            f_hi = fg.astype(BF16).astype(F32)
            f_mid = (fg - f_hi).astype(BF16).astype(F32)
            f_lo = fg - f_hi - f_mid
            a = jnp.where(row == 0, f_hi, jnp.where(row == 1, f_mid, jnp.where(
                row == 2, f_lo, jnp.where(row == 3, k, jnp.where(row == 4, q, 0.0)))))
            bmat = jnp.concatenate([ones_f, ones_q, jnp.where(row == 3, v, 0.0)], axis=1)
            splats.append(lax.dot_general(a.astype(BF16), bmat.astype(BF16), _TN,
                                          preferred_element_type=F32))
        for h, gm in enumerate(splats):
            s_new = gm[:, 0:HEAD] * s_ref[r, h] + gm[:, 2 * HEAD:3 * HEAD]
            so_ref[r, h] = s_new
            o = jnp.sum(gm[:, HEAD:2 * HEAD] * s_new, axis=0, keepdims=True)
            og = zrow[:, 3 * A_WIDTH + h * HEAD: 3 * A_WIDTH + (h + 1) * HEAD]
            o_ref[r, :, h * HEAD:(h + 1) * HEAD] = _hgrn_out(
                o, g_ref[:, h * HEAD:(h + 1) * HEAD], og)
        return carry

    lax.fori_loop(0, SAMPLE_ROWS, per_seq, 0, unroll=SAMPLE_UNROLL)


def _hgrn_sample(z_small, state, lb_logits, hgrn_g):
    nb = state.shape[0]
    blocks = SAMPLE_ROWS * (IN_COLS * 4 + A_WIDTH * 4 + 2 * A_HEADS * HEAD * HEAD * 4)
    return pl.pallas_call(
        _hgrn_sample_kernel,
        grid=(nb // SAMPLE_ROWS,),
        in_specs=[pl.BlockSpec((SAMPLE_ROWS, 1, IN_COLS), lambda i: (i, 0, 0)),
                  pl.BlockSpec((SAMPLE_ROWS, A_HEADS, HEAD, HEAD), lambda i: (i, 0, 0, 0)),
                  pl.BlockSpec(lb_logits.shape, lambda i: (0, 0)),
                  pl.BlockSpec((1, A_WIDTH), lambda i: (0, 0))],
        out_specs=[pl.BlockSpec((SAMPLE_ROWS, 1, A_WIDTH), lambda i: (i, 0, 0)),
                   pl.BlockSpec((SAMPLE_ROWS, A_HEADS, HEAD, HEAD), lambda i: (i, 0, 0, 0))],
        out_shape=[jax.ShapeDtypeStruct((nb, 1, A_WIDTH), F32),
                   jax.ShapeDtypeStruct(state.shape, F32)],
        compiler_params=_params(_vmem_limit(blocks, 2 * 1024 * 1024), 1),
        name="hgrn_sample",
    )(z_small, state, lb_logits, hgrn_g)


def _group_norm_swish(c, gn_g, gn_b):
    mu = jnp.mean(c, axis=-1, keepdims=True)
    d = c - mu
    var = jnp.mean(d * d, axis=-1, keepdims=True)
    return _silu(d * lax.rsqrt(var + EPS) * gn_g + gn_b)


CONV_PAD = 32


CONV_SEQS = 16


def _conv_sample_kernel(ga_ref, gb_ref, st_ref, w_ref, cb_ref, gg_ref, gbias_ref, o_ref, buf_ref):
    u = ga_ref[...] * _sigmoid(gb_ref[...])
    c = w_ref[HIST:CONV_W, :] * u + cb_ref[...]
    for j in range(HIST):
        c = c + w_ref[j:j + 1, :] * st_ref[j]
    for g in range(B_GROUPS):
        sl = slice(g * GROUP, (g + 1) * GROUP)
        o_ref[:, sl] = _group_norm_swish(c[:, sl], gg_ref[:, sl], gbias_ref[:, sl])
    for j in range(HIST - 1):
        buf_ref[j] = st_ref[j + 1]
    buf_ref[HIST - 1] = u


def _conv_sample(z_small, state_t, conv_w, conv_b, gn_g, gn_b):
    nb = state_t.shape[1]
    ga0 = 4 * A_WIDTH // B_WIDTH
    vec = pl.BlockSpec((1, B_WIDTH), lambda i: (0, 0))
    blocks = CONV_SEQS * B_WIDTH * 4 * (3 + 2 * HIST) + 32 * B_WIDTH * 4
    return pl.pallas_call(
        _conv_sample_kernel,
        grid=(nb // CONV_SEQS,),
        in_specs=[pl.BlockSpec((CONV_SEQS, B_WIDTH), lambda i: (i, ga0)),
                  pl.BlockSpec((CONV_SEQS, B_WIDTH), lambda i: (i, ga0 + 1)),
                  pl.BlockSpec((HIST, CONV_SEQS, B_WIDTH), lambda i: (0, i, 0)),
                  pl.BlockSpec((CONV_W, B_WIDTH), lambda i: (0, 0)),
                  vec, vec, vec],
        out_specs=[pl.BlockSpec((CONV_SEQS, B_WIDTH), lambda i: (i, 0)),
                   pl.BlockSpec((HIST, CONV_SEQS, B_WIDTH), lambda i: (0, i, 0))],
        out_shape=[jax.ShapeDtypeStruct((nb, B_WIDTH), F32),
                   jax.ShapeDtypeStruct(state_t.shape, F32)],
        compiler_params=_params(_vmem_limit(blocks, 2 * 1024 * 1024), 1),
        name="conv_sample",
    )(z_small, z_small, state_t, conv_w, conv_b, gn_g, gn_b)


def _outproj_rows(oa_ref, ob_ref, x_ref, wa_ref, wb_ref, g_ref, h_ref, hf_ref):
    h = x_ref[...]
    h = h + jnp.dot(oa_ref[...].astype(BF16), wa_ref[...], preferred_element_type=F32)
    h = h + jnp.dot(ob_ref[...].astype(BF16), wb_ref[...], preferred_element_type=F32)
    h_ref[...] = h
    hf_ref[...] = _rms_rows(h, g_ref[...]).astype(BF16)


def _outproj_kernel(oa_ref, ob_ref, x_ref, oas_ref, obs_ref, xs_ref, wa_ref, wb_ref, g_ref,
                    h_ref, hf_ref, hs_ref, hfs_ref):
    _outproj_rows(oa_ref, ob_ref, x_ref, wa_ref, wb_ref, g_ref, h_ref, hf_ref)

    @pl.when(pl.program_id(0) == 0)
    def _():
        _outproj_rows(oas_ref, obs_ref, xs_ref, wa_ref, wb_ref, g_ref, hs_ref, hfs_ref)


def _outproj(oa, ob, x, oa_s, ob_s, x_s, w_bf16, g, tm):
    m, ms = x.shape[0], x_s.shape[0]
    blocks = (2 * tm * A_WIDTH * oa.dtype.itemsize + tm * D_MODEL * (4 + 4 + 2) + D_MODEL * D_MODEL * 2
              + 2 * ms * A_WIDTH * oa_s.dtype.itemsize + ms * D_MODEL * (4 + 4 + 2))
    whole = lambda cols: pl.BlockSpec((ms, cols), lambda i: (0, 0))
    return pl.pallas_call(
        _outproj_kernel,
        grid=(m // tm,),
        in_specs=[pl.BlockSpec((tm, A_WIDTH), lambda i: (i, 0)),
                  pl.BlockSpec((tm, B_WIDTH), lambda i: (i, 0)),
                  pl.BlockSpec((tm, D_MODEL), lambda i: (i, 0)),
                  whole(A_WIDTH), whole(B_WIDTH), whole(D_MODEL),
                  pl.BlockSpec((A_WIDTH, D_MODEL), lambda i: (0, 0)),
                  pl.BlockSpec((B_WIDTH, D_MODEL), lambda i: (1, 0)),
                  pl.BlockSpec((1, D_MODEL), lambda i: (0, 0))],
        out_specs=[pl.BlockSpec((tm, D_MODEL), lambda i: (i, 0)),
                   pl.BlockSpec((tm, D_MODEL), lambda i: (i, 0)),
                   whole(D_MODEL), whole(D_MODEL)],
        out_shape=[jax.ShapeDtypeStruct((m, D_MODEL), F32),
                   jax.ShapeDtypeStruct((m, D_MODEL), BF16),
                   jax.ShapeDtypeStruct((ms, D_MODEL), F32),
                   jax.ShapeDtypeStruct((ms, D_MODEL), BF16)],
        compiler_params=_params(_vmem_limit(blocks, 2 * tm * D_MODEL * 4), 1),
        name="outproj",
    )(oa, ob, x, oa_s, ob_s, x_s, w_bf16, w_bf16, g)


RESID_ROWS = 128


def _swiglu_down(hf, wg_ref, wu_ref, wd_ref):
    gate = jnp.dot(hf, wg_ref[...], preferred_element_type=F32)
    up = jnp.dot(hf, wu_ref[...], preferred_element_type=F32)
    return jnp.dot((_silu(gate) * up).astype(BF16), wd_ref[...], preferred_element_type=F32)


def _ffn_kernel(hf_ref, h_ref, hfs_ref, hs_ref, wg_ref, wu_ref, wd_ref, g_ref, y_ref, ys_ref):
    f = pl.program_id(1)
    last = pl.num_programs(1) - 1
    resid = h_ref.shape[0]

    @pl.when(f == 0)
    def _():
        y_ref[...] = jnp.zeros(y_ref.shape, F32)

    y_ref[...] += _swiglu_down(hf_ref[...], wg_ref, wu_ref, wd_ref)

    @pl.when(f < y_ref.shape[0] // resid)
    def _():
        rows = pl.ds(pl.multiple_of(f * resid, resid), resid)
        y_ref[rows, :] += h_ref[...]

    @pl.when(f == last)
    def _():
        y_ref[...] = _rms_rows(y_ref[...], g_ref[...])

    @pl.when(pl.program_id(0) == 0)
    def _():
        @pl.when(f == 0)
        def _():
            ys_ref[...] = hs_ref[...]

        ys_ref[...] += _swiglu_down(hfs_ref[...], wg_ref, wu_ref, wd_ref)

        @pl.when(f == last)
        def _():
            ys_ref[...] = _rms_rows(ys_ref[...], g_ref[...])


def _ffn(hf, h1, hf_s, h1_s, wg, wu, wd, g, tm, tf):
    m, ms = hf.shape[0], hf_s.shape[0]
    resid = min(RESID_ROWS, tm)
    slabs = tm // resid
    assert tm % resid == 0 and slabs <= D_FF // tf
    blocks = (tm * D_MODEL * (2 + 4) + resid * D_MODEL * 4 + 3 * D_MODEL * tf * 2
              + ms * D_MODEL * (2 + 4 + 4))
    resident = tm * D_MODEL * 4 + 3 * tm * tf * 4
    whole = lambda rows: pl.BlockSpec((rows, D_MODEL), lambda i, f: (0, 0))
    return pl.pallas_call(
        _ffn_kernel,
        grid=(m // tm, D_FF // tf),
        in_specs=[pl.BlockSpec((tm, D_MODEL), lambda i, f: (i, 0)),
                  pl.BlockSpec((resid, D_MODEL), lambda i, f: (i * slabs + jnp.minimum(f, slabs - 1), 0)),
                  whole(ms), whole(ms),
                  pl.BlockSpec((D_MODEL, tf), lambda i, f: (0, f)),
                  pl.BlockSpec((D_MODEL, tf), lambda i, f: (0, f)),
                  pl.BlockSpec((tf, D_MODEL), lambda i, f: (f, 0)),
                  pl.BlockSpec((1, D_MODEL), lambda i, f: (0, 0))],
        out_specs=[pl.BlockSpec((tm, D_MODEL), lambda i, f: (i, 0)), whole(ms)],
        out_shape=[jax.ShapeDtypeStruct((m, D_MODEL), F32),
                   jax.ShapeDtypeStruct((ms, D_MODEL), F32)],
        compiler_params=_params(_vmem_limit(blocks, resident), 2),
        name="ffn",
    )(hf, h1, hf_s, h1_s, wg, wu, wd, g)


def kernel(x_prompt, x_sample, state_hgrn, state_conv, meta_tokens, norm_mix_g, w_in, lb_logits,
           hgrn_norm_g, conv_w, conv_b, gn_g, gn_b, w_out, norm_ffn_g, w_ffn_gate, w_ffn_up,
           w_ffn_down, norm_final_g):
    batch, seq, _ = x_prompt.shape
    nb = x_sample.shape[0]
    assert x_sample.shape[1] == 1 and norm_mix_g.shape[0] == 1
    assert TIME_TILE % (CHUNK * CHUNK_UNROLL) == 0
    assert nb % N_META == 0 and nb % CONV_SEQS == 0 and nb % SAMPLE_ROWS == 0

    w_in_b = w_in[0].astype(BF16)
    g_final = norm_final_g[None, :]

    xp = x_prompt.reshape(batch * seq, D_MODEL)
    xs = x_sample.reshape(nb, D_MODEL)
    x_small = jnp.concatenate([xs, meta_tokens], axis=0)

    z, z_small = _inproj(xp, x_small, norm_mix_g, w_in_b, tm=1024, tn=1536)

    oa_p, s_p, ob_p, c_p, w_out_b, wg_b, wu_b, wd_b = _mixers(
        z, z_small, lb_logits, hgrn_norm_g, conv_w[0], conv_b, gn_g, gn_b,
        w_out[0], w_ffn_gate[0], w_ffn_up[0], w_ffn_down[0], batch, seq)
    z_s = z_small[:nb].reshape(nb, 1, IN_COLS)
    oa_s, s_s = _hgrn_sample(z_s, state_hgrn[0], lb_logits, hgrn_norm_g)
    ob_s, c_s = _conv_sample(z_small, jnp.transpose(state_conv[0], (1, 0, 2)), conv_w[0], conv_b, gn_g, gn_b)
    c_s = jnp.transpose(c_s, (1, 0, 2))
    oa_s = oa_s.reshape(nb, A_WIDTH)

    h_p, hf_p, h_s, hf_s = _outproj(oa_p, ob_p, xp, oa_s, ob_s, xs, w_out_b, norm_ffn_g, tm=512)

    y_p, y_s = _ffn(hf_p, h_p, hf_s, h_s, wg_b, wu_b, wd_b, g_final, tm=1024, tf=512)

    return (y_p.reshape(batch, seq, D_MODEL), y_s.reshape(nb, 1, D_MODEL),
            s_p[None], c_p[None], s_s[None], c_s[None])
```
